```python
import functools
import jax, jax.numpy as jnp
from jax import lax
import numpy as np

D_MODEL = 1024
BATCH = 2
SEQ = 8192
DEPTH = 2
DEC_BATCH = 128
DEC_SEQ = 1
PAST_LEN = 2048
PAGE_SIZE = 128

POOL_WINDOWS = (2, 4, 8, 16)
POOL_GROUPS = 4
POOL_GROUP = D_MODEL // 8
POOL_WIDTH = POOL_GROUPS * POOL_GROUP
POOL_BUF = max(POOL_WINDOWS) - 1
CHUNK = 128
SG_GROUPS = 4
SG_GROUP = D_MODEL // 8
SG_WIDTH = SG_GROUPS * SG_GROUP
SB_HEADS = 8
SB_HEAD_DIM = D_MODEL // 16
SB_WIDTH = SB_HEADS * SB_HEAD_DIM
SB_BLOCK = 128
SB_SCALE = SB_HEAD_DIM ** -0.5
SB_BIAS_INIT = -5.0
N_BRANCH = 3
BRANCH_WIDTH = 512
IN_SPLITS = (POOL_WIDTH,
             POOL_WIDTH + SG_WIDTH,
             POOL_WIDTH + 2 * SG_WIDTH,
             POOL_WIDTH + 2 * SG_WIDTH + SB_WIDTH,
             POOL_WIDTH + 2 * SG_WIDTH + 2 * SB_WIDTH,
             POOL_WIDTH + 2 * SG_WIDTH + 3 * SB_WIDTH)
IN_WIDTH = IN_SPLITS[-1] + N_BRANCH * D_MODEL
FFN_DIM = ((8 * D_MODEL // 3 + 127) // 128) * 128
CONV_WIDTH = 3
EPS = 1e-6

kernel_name = 'hybrid_pool_sgmlp_stickbreak_decode_step'


def rms_norm(x, g):
    xf = x.astype(jnp.float32)
    y = xf * lax.rsqrt(jnp.mean(xf * xf, axis=-1, keepdims=True) + EPS)
    return (y * g.astype(jnp.float32)).astype(x.dtype)


def pool_mixer(xa, prefix, pos0, pool_w, pool_scale):
    B, T, _ = xa.shape
    ext = jnp.concatenate([prefix.astype(xa.dtype), xa], axis=1)
    c = jnp.cumsum(ext.astype(jnp.float32), axis=1)
    c = jnp.pad(c, ((0, 0), (1, 0), (0, 0)))
    pos = (pos0 + jnp.arange(T)).astype(jnp.float32)
    groups = []
    for gi, w in enumerate(POOL_WINDOWS):
        sl = slice(gi * POOL_GROUP, (gi + 1) * POOL_GROUP)
        s = c[:, POOL_BUF + 1:POOL_BUF + 1 + T, sl] - c[:, POOL_BUF + 1 - w:POOL_BUF + 1 - w + T, sl]
        cnt = jnp.minimum(pos + 1.0, float(w))[None, :, None]
        groups.append(s / cnt)
    mean = jnp.stack(groups, axis=2)
    d = (mean - xa.reshape(B, T, POOL_GROUPS, POOL_GROUP).astype(jnp.float32)).astype(xa.dtype)
    y = jnp.einsum('btgc,gcd->btgd', d, pool_w).reshape(B, T, POOL_WIDTH)
    return y * pool_scale, ext[:, -POOL_BUF:]


def spatial_gate(u, v, sg_w, sg_b):
    B, T, _ = v.shape
    L = min(T, CHUNK)
    n = T // L
    w = jnp.tril(sg_w[:, :L, :L])
    vc = v.reshape(B, n, L, SG_GROUPS, SG_GROUP)
    mixed = jnp.einsum('gts,bnsgc->bntgc', w, vc) + sg_b[:, :L].T[None, None, :, :, None]
    return u * mixed.reshape(B, T, SG_WIDTH)


def stick_breaking(q, k, v, bias, qpos):
    kpos = jnp.arange(k.shape[1])
    z = (jnp.einsum('bqhd,bkhd->bhqk', q, k).astype(jnp.float32) * SB_SCALE
         + bias.astype(jnp.float32)[None, :, None, None])
    mask = kpos[None, :] < qpos[:, None]
    log_rest = jnp.where(mask, jax.nn.log_sigmoid(-z), 0.0)
    after = lax.cumsum(log_rest, axis=3, reverse=True) - log_rest
    a = jnp.where(mask, jnp.exp(jax.nn.log_sigmoid(z) + after), 0.0)
    return jnp.einsum('bhqk,bkhd->bqhd', a.astype(v.dtype), v)


def sb_prompt(q, k, v, bias):
    B, S, H, Dh = q.shape
    nb = S // SB_BLOCK
    qb = q.reshape(B, nb, SB_BLOCK, H, Dh).transpose(1, 0, 2, 3, 4)

    def one(args):
        qi, i = args
        return stick_breaking(qi, k, v, bias, i * SB_BLOCK + jnp.arange(SB_BLOCK))

    out = lax.map(one, (qb, jnp.arange(nb)))
    return out.transpose(1, 0, 2, 3, 4).reshape(B, S, H, Dh)


def sb_sample(q, k, v, bias, k_past, v_past, past_len):
    kk = jnp.concatenate([k_past.astype(k.dtype), k], axis=1)
    vv = jnp.concatenate([v_past.astype(v.dtype), v], axis=1)
    return stick_breaking(q, kk, vv, bias, past_len + jnp.arange(q.shape[1]))


def conv_ffn(h, prefix, w_up, conv_w, conv_b, w_down):
    up = h @ w_up
    T = up.shape[1]
    ext = jnp.concatenate([prefix.astype(up.dtype), up], axis=1)
    c = conv_b + sum(ext[:, j:j + T] * conv_w[j] for j in range(CONV_WIDTH))
    gate, val = jnp.split(c, 2, axis=-1)
    return (jax.nn.gelu(gate) * val) @ w_down, ext[:, -(CONV_WIDTH - 1):]


def layer(x, pos0, pool_prefix, conv_prefix, attend, p):
    (g_mix_pre, g_mix_post, g_ffn_pre, g_ffn_post, w_in, pool_w, pool_scale,
     sg_w, sg_b, sb_bias, w_branch, w_out, w_up, conv_w, conv_b, w_down) = p
    B, T, _ = x.shape
    h = rms_norm(x, g_mix_pre)
    a, u, v, q, k, vv, gates = jnp.split(h @ w_in, IN_SPLITS, axis=-1)
    ya, pool_state = pool_mixer(a, pool_prefix, pos0, pool_w, pool_scale)
    yb = spatial_gate(u, v, sg_w, sg_b)
    q = q.reshape(B, T, SB_HEADS, SB_HEAD_DIM)
    k = k.reshape(B, T, SB_HEADS, SB_HEAD_DIM)
    vv = vv.reshape(B, T, SB_HEADS, SB_HEAD_DIM)
    yc = attend(q, k, vv, sb_bias).reshape(B, T, SB_WIDTH)
    br = jnp.einsum('btnc,ncd->btnd', jnp.stack([ya, yb, yc], axis=2), w_branch)
    g = jax.nn.sigmoid(gates.reshape(B, T, N_BRANCH, D_MODEL))
    mixed = jnp.sum(g * br, axis=2) @ w_out
    x = x + rms_norm(mixed, g_mix_post)
    f, conv_state = conv_ffn(rms_norm(x, g_ffn_pre), conv_prefix, w_up, conv_w, conv_b, w_down)
    x = x + rms_norm(f, g_ffn_post)
    return x, pool_state, k, vv, v, conv_state


def setup_inputs(seed: int = 0) -> dict:
    key = jax.random.key(seed)
    ks = jax.random.split(key, 23)
    n_pages = PAST_LEN // PAGE_SIZE
    n_used = DEC_BATCH * n_pages
    n_pool = n_used + n_used // 4

    def nrm(k, shape, scale):
        return jax.random.normal(k, shape, jnp.float32) * scale

    def gain(k):
        return 1.0 + nrm(k, (DEPTH, D_MODEL), 0.05)

    page_table = jax.random.permutation(ks[5], n_pool)[:n_used].reshape(DEC_BATCH, n_pages).astype(jnp.int32)
    return {
        'x_prompt': nrm(ks[0], (BATCH, SEQ, D_MODEL), 1.0),
        'x_sample': nrm(ks[1], (DEC_BATCH, DEC_SEQ, D_MODEL), 1.0),
        'state_pool': nrm(ks[2], (DEPTH, DEC_BATCH, POOL_BUF, POOL_WIDTH), 1.0),
        'cache_k': nrm(ks[3], (DEPTH, n_pool, PAGE_SIZE, SB_HEADS, SB_HEAD_DIM), 1.0),
        'cache_v': nrm(ks[4], (DEPTH, n_pool, PAGE_SIZE, SB_HEADS, SB_HEAD_DIM), 1.0),
        'page_table': page_table,
        'state_conv': nrm(ks[6], (DEPTH, DEC_BATCH, CONV_WIDTH - 1, 2 * FFN_DIM), 1.0),
        'norm_mix_pre': gain(ks[7]),
        'norm_mix_post': gain(ks[8]),
        'norm_ffn_pre': gain(ks[9]),
        'norm_ffn_post': gain(ks[10]),
        'w_in': nrm(ks[11], (DEPTH, D_MODEL, IN_WIDTH), D_MODEL ** -0.5),
        'pool_w': nrm(ks[12], (DEPTH, POOL_GROUPS, POOL_GROUP, POOL_GROUP), POOL_GROUP ** -0.5),
        'pool_scale': 1.0 + nrm(ks[13], (DEPTH, POOL_WIDTH), 0.05),
        'sg_w': nrm(ks[14], (DEPTH, SG_GROUPS, CHUNK, CHUNK), CHUNK ** -0.5),
        'sg_b': 1.0 + nrm(ks[15], (DEPTH, SG_GROUPS, CHUNK), 0.05),
        'sb_bias': SB_BIAS_INIT + nrm(ks[22], (DEPTH, SB_HEADS), 0.1),
        'w_branch': nrm(ks[16], (DEPTH, N_BRANCH, BRANCH_WIDTH, D_MODEL), BRANCH_WIDTH ** -0.5),
        'w_out': nrm(ks[17], (DEPTH, D_MODEL, D_MODEL), D_MODEL ** -0.5),
        'w_up': nrm(ks[18], (DEPTH, D_MODEL, 2 * FFN_DIM), D_MODEL ** -0.5),
        'conv_w': nrm(ks[19], (DEPTH, CONV_WIDTH, 2 * FFN_DIM), CONV_WIDTH ** -0.5),
        'conv_b': nrm(ks[20], (DEPTH, 2 * FFN_DIM), 0.02),
        'w_down': nrm(ks[21], (DEPTH, FFN_DIM, D_MODEL), FFN_DIM ** -0.5),
    }


def reference(x_prompt, x_sample, state_pool, cache_k, cache_v, page_table, state_conv,
              norm_mix_pre, norm_mix_post, norm_ffn_pre, norm_ffn_post, w_in, pool_w,
              pool_scale, sg_w, sg_b, sb_bias, w_branch, w_out, w_up, conv_w, conv_b, w_down):
    B = x_prompt.shape[0]
    DB = x_sample.shape[0]
    past_len = page_table.shape[1] * cache_k.shape[2]
    weights = (norm_mix_pre, norm_mix_post, norm_ffn_pre, norm_ffn_post, w_in, pool_w,
               pool_scale, sg_w, sg_b, sb_bias, w_branch, w_out, w_up, conv_w, conv_b, w_down)
    xp, xs = x_prompt, x_sample
    pool_p, pool_s, k_p, v_p, k_s, v_s, sgv_s, conv_p, conv_s = ([] for _ in range(9))
    for l in range(DEPTH):
        p = tuple(w[l] for w in weights)
        zero_pool = jnp.zeros((B, POOL_BUF, POOL_WIDTH), xp.dtype)
        zero_conv = jnp.zeros((B, CONV_WIDTH - 1, 2 * FFN_DIM), xp.dtype)
        xp, pp, kp, vp, _, cp = layer(xp, 0, zero_pool, zero_conv, sb_prompt, p)
        k_past = cache_k[l][page_table].reshape(DB, past_len, SB_HEADS, SB_HEAD_DIM)
        v_past = cache_v[l][page_table].reshape(DB, past_len, SB_HEADS, SB_HEAD_DIM)
        attend = functools.partial(sb_sample, k_past=k_past, v_past=v_past, past_len=past_len)
        xs, ps, ks_, vs_, sgv, cs = layer(xs, past_len, state_pool[l], state_conv[l], attend, p)
        pool_p.append(pp); pool_s.append(ps)
        k_p.append(kp); v_p.append(vp); k_s.append(ks_); v_s.append(vs_)
        sgv_s.append(sgv); conv_p.append(cp); conv_s.append(cs)
    return (xp, xs, jnp.stack(pool_p), jnp.stack(pool_s), jnp.stack(k_p), jnp.stack(v_p),
            jnp.stack(k_s), jnp.stack(v_s), jnp.stack(sgv_s), jnp.stack(conv_p), jnp.stack(conv_s))
```

```python
import functools

import jax
import jax.numpy as jnp
from jax import lax
from jax.experimental import pallas as pl
from jax.experimental.pallas import tpu as pltpu

F32 = jnp.float32
BF16 = jnp.bfloat16

LANES = 128
POOL_WINDOWS = (2, 4, 8, 16)
POOL_PREFIX = max(POOL_WINDOWS)
POOL_BUF = POOL_PREFIX - 1
CHUNK = 128
SB_HEADS = 8
SB_HEAD_DIM = 64
SB_WIDTH = SB_HEADS * SB_HEAD_DIM
SB_SCALE = SB_HEAD_DIM ** -0.5
KEY_BLOCK = 128
CONV_WIDTH = 3
CONV_PREFIX = 8
EPS = 1e-6
VMEM_LIMIT = 56 * 1024 * 1024

TOKEN_TILE = 256
QUERY_TILE = 256
FFN_CHUNK = 256


def _rms(x, g):
    return x * lax.rsqrt(jnp.mean(x * x, axis=-1, keepdims=True) + EPS) * g


def _softplus(z):
    return jnp.maximum(z, 0.0) + jnp.log(1.0 + jnp.exp(-jnp.abs(z)))


def _gelu_tanh(x):
    return 0.5 * x * (1.0 + jnp.tanh(0.7978845608028654 * (x + 0.044715 * (x * x * x))))


def _const_spec(shape):
    nd = len(shape)
    return pl.BlockSpec(shape, lambda *_: (0,) * nd, pipeline_mode=pl.Buffered(1))


def _suffix_sum_matrix():
    r = lax.broadcasted_iota(jnp.int32, (KEY_BLOCK, KEY_BLOCK), 0)
    c = lax.broadcasted_iota(jnp.int32, (KEY_BLOCK, KEY_BLOCK), 1)
    t = (r > c).astype(BF16)
    return jnp.concatenate([t, jnp.ones((KEY_BLOCK, KEY_BLOCK), BF16)], axis=1)


def _head_mask(shape, lane_axis, head_axis):
    lane = lax.broadcasted_iota(jnp.int32, shape, lane_axis)
    head = lax.broadcasted_iota(jnp.int32, shape, head_axis)
    return lane // SB_HEAD_DIM == head


def _prompt_in_kernel(x_ref, g_ref, win_ref, poolw_ref, pscale_ref, sgw_ref, sgb_ref, wbr_ref,
                      q_ref, k_ref, v_ref, kb_ref, vb_ref, mab_ref, gc_ref, pool_ref,
                      ext_ref, *, tiles_per_seq):
    tm = x_ref.shape[0]
    i = pl.program_id(0)
    t0 = (i % tiles_per_seq) * tm

    @pl.when(i % tiles_per_seq == 0)
    def _():
        ext_ref[0:POOL_PREFIX, :] = jnp.zeros((POOL_PREFIX, ext_ref.shape[1]), F32)

    h = _rms(x_ref[...], g_ref[...]).astype(BF16)

    def proj(lo, hi):
        return jnp.dot(h, win_ref[:, lo:hi], preferred_element_type=F32)

    ext_ref[POOL_PREFIX:, :] = proj(0, 512)
    pos1 = (t0 + 1 + lax.broadcasted_iota(jnp.int32, (tm, LANES), 0)).astype(F32)
    ya = []
    for gi, w in enumerate(POOL_WINDOWS):
        sl = slice(gi * LANES, (gi + 1) * LANES)
        cur = ext_ref[POOL_PREFIX:POOL_PREFIX + tm, sl]
        s = cur
        for j in range(1, w):
            s = s + ext_ref[POOL_PREFIX - j:POOL_PREFIX - j + tm, sl]
        d = (s / jnp.minimum(pos1, float(w)) - cur).astype(BF16)
        ya.append(jnp.dot(d, poolw_ref[gi], preferred_element_type=F32) * pscale_ref[:, sl])
    ya = jnp.concatenate(ya, axis=1).astype(BF16)
    last = ext_ref[tm:tm + POOL_PREFIX, :]
    pool_ref[0] = last
    ext_ref[0:POOL_PREFIX, :] = last

    u = proj(512, 1024)
    sv = proj(1024, 1536).astype(BF16)
    r = lax.broadcasted_iota(jnp.int32, (CHUNK, CHUNK), 0)
    c = lax.broadcasted_iota(jnp.int32, (CHUNK, CHUNK), 1)
    cols = []
    for gi in range(4):
        sl = slice(gi * LANES, (gi + 1) * LANES)
        wg = jnp.where(r >= c, sgw_ref[gi], 0.0).astype(BF16)
        bcol = sgb_ref[:, gi:gi + 1]
        rows = []
        for ci in range(tm // CHUNK):
            rs = slice(ci * CHUNK, (ci + 1) * CHUNK)
            mixed = jnp.dot(wg, sv[rs, sl], preferred_element_type=F32) + bcol
            rows.append(u[rs, sl] * mixed)
        cols.append(jnp.concatenate(rows, axis=0))
    yb = jnp.concatenate(cols, axis=1).astype(BF16)

    q_ref[...] = (proj(1536, 2048) * SB_SCALE).astype(BF16)
    k = proj(2048, 2560)
    k_ref[...] = k
    kb_ref[...] = k.astype(BF16)
    v = proj(2560, 3072)
    v_ref[...] = v
    vb_ref[...] = v.astype(BF16)

    ga = jax.nn.sigmoid(proj(3072, 4096))
    gb = jax.nn.sigmoid(proj(4096, 5120))
    gc_ref[...] = jax.nn.sigmoid(proj(5120, 6144)).astype(BF16)
    mab_ref[...] = (ga * jnp.dot(ya, wbr_ref[0], preferred_element_type=F32)
                    + gb * jnp.dot(yb, wbr_ref[1], preferred_element_type=F32))


def _prompt_in(x2, g, win, poolw, pscale, sgw, sgb_t, wbr_ab, *, batch, seq):
    n, d = x2.shape
    tm = TOKEN_TILE
    tps = seq // tm
    row = lambda w: pl.BlockSpec((tm, w), lambda i: (i, 0))
    out_shape = (
        jax.ShapeDtypeStruct((n, SB_WIDTH), BF16),
        jax.ShapeDtypeStruct((n, SB_WIDTH), F32),
        jax.ShapeDtypeStruct((n, SB_WIDTH), F32),
        jax.ShapeDtypeStruct((n, SB_WIDTH), BF16),
        jax.ShapeDtypeStruct((n, SB_WIDTH), BF16),
        jax.ShapeDtypeStruct((n, d), F32),
        jax.ShapeDtypeStruct((n, d), BF16),
        jax.ShapeDtypeStruct((batch, POOL_PREFIX, 512), F32),
    )
    out_specs = (row(SB_WIDTH), row(SB_WIDTH), row(SB_WIDTH), row(SB_WIDTH), row(SB_WIDTH),
                 row(d), row(d),
                 pl.BlockSpec((1, POOL_PREFIX, 512), lambda i: (i // tps, 0, 0)))
    return pl.pallas_call(
        functools.partial(_prompt_in_kernel, tiles_per_seq=tps),
        grid=(n // tm,),
        in_specs=[row(d), _const_spec(g.shape), _const_spec(win.shape), _const_spec(poolw.shape),
                  _const_spec(pscale.shape), _const_spec(sgw.shape), _const_spec(sgb_t.shape),
                  _const_spec(wbr_ab.shape)],
        out_specs=out_specs,
        out_shape=out_shape,
        scratch_shapes=[pltpu.VMEM((tm + POOL_PREFIX, 512), F32)],
        compiler_params=pltpu.CompilerParams(dimension_semantics=("arbitrary",),
                                             vmem_limit_bytes=VMEM_LIMIT),
        name="prompt_in",
    )(x2, g, win, poolw, pscale, sgw, sgb_t, wbr_ab)


def _prompt_attn_kernel(bias_ref, q_ref, kb_ref, vb_ref, t2_ref, mab_ref, gc_ref, x_ref,
                        wbrc_ref, wout_ref, g_ref, o_ref, qm_ref, acc_ref, carry_ref):
    tq = q_ref.shape[1]
    qi = pl.program_id(1)
    q0 = qi * tq
    n_diag = tq // KEY_BLOCK

    lane = lax.broadcasted_iota(jnp.int32, (tq, LANES), 1)
    low = lane < SB_HEAD_DIM
    for p in range(SB_HEADS // 2):
        qt = q_ref[0, :, p * LANES:(p + 1) * LANES]
        qm_ref[2 * p] = jnp.where(low, qt, 0.0).astype(BF16)
        qm_ref[2 * p + 1] = jnp.where(low, 0.0, qt).astype(BF16)
    acc_ref[...] = jnp.zeros(acc_ref.shape, F32)
    carry_ref[...] = jnp.zeros(carry_ref.shape, F32)

    rel = lax.broadcasted_iota(jnp.int32, (tq, KEY_BLOCK), 1) - lax.broadcasted_iota(
        jnp.int32, (tq, KEY_BLOCK), 0)

    def block(k0, mask):
        for h in range(SB_HEADS):
            p = h // 2
            kt = kb_ref[0, pl.ds(k0, KEY_BLOCK), p * LANES:(p + 1) * LANES]
            vt = vb_ref[0, pl.ds(k0, KEY_BLOCK), p * LANES:(p + 1) * LANES]
            z = lax.dot_general(qm_ref[h], kt, (((1,), (1,)), ((), ())),
                                preferred_element_type=F32) + bias_ref[h]
            sp = _softplus(z)
            if mask is not None:
                sp = jnp.where(mask, sp, 0.0)
            r = jnp.dot(sp.astype(BF16), t2_ref[...], preferred_element_type=F32)
            w = z - sp - r[:, :KEY_BLOCK] - carry_ref[h]
            a = jnp.exp(w)
            if mask is not None:
                a = jnp.where(mask, a, 0.0)
            acc_ref[h] += jnp.dot(a.astype(BF16), vt, preferred_element_type=F32)
            carry_ref[h] += r[:, KEY_BLOCK:]

    for t in range(n_diag):
        off = tq - (t + 1) * KEY_BLOCK
        block(pl.multiple_of(q0 + off, KEY_BLOCK), rel < -off)

    def body(t, _):
        block(pl.multiple_of(q0 - (t + 1) * KEY_BLOCK, KEY_BLOCK), None)
        return 0

    lax.fori_loop(0, q0 // KEY_BLOCK, body, 0)

    tiles = []
    for p in range(SB_HEADS // 2):
        tiles.append(jnp.where(low, acc_ref[2 * p], acc_ref[2 * p + 1]))
    yc = jnp.concatenate(tiles, axis=1).astype(BF16)
    m = mab_ref[0] + gc_ref[0].astype(F32) * jnp.dot(yc, wbrc_ref[...], preferred_element_type=F32)
    mixed = jnp.dot(m.astype(BF16), wout_ref[...], preferred_element_type=F32)
    o_ref[0] = x_ref[0] + _rms(mixed, g_ref[...])


def _prompt_attn(bias, q, kb, vb, t2, mab, gc, x, wbr_c, wout, g_post):
    b, s, d = x.shape
    tq = QUERY_TILE
    row = lambda w: pl.BlockSpec((1, tq, w), lambda bi, qi: (bi, qi, 0))
    seq_spec = pl.BlockSpec((1, s, SB_WIDTH), lambda bi, qi: (bi, 0, 0), pipeline_mode=pl.Buffered(1))
    return pl.pallas_call(
        _prompt_attn_kernel,
        grid=(b, s // tq),
        in_specs=[pl.BlockSpec(memory_space=pltpu.SMEM),
                  row(SB_WIDTH), seq_spec, seq_spec, _const_spec(t2.shape),
                  row(d), row(d), row(d),
                  _const_spec(wbr_c.shape), _const_spec(wout.shape), _const_spec(g_post.shape)],
        out_specs=row(d),
        out_shape=jax.ShapeDtypeStruct((b, s, d), F32),
        scratch_shapes=[pltpu.VMEM((SB_HEADS, tq, LANES), BF16),
                        pltpu.VMEM((SB_HEADS, tq, LANES), F32),
                        pltpu.VMEM((SB_HEADS, tq, LANES), F32)],
        compiler_params=pltpu.CompilerParams(dimension_semantics=("arbitrary", "arbitrary"),
                                             vmem_limit_bytes=VMEM_LIMIT),
        name="prompt_attn",
    )(bias, q, kb, vb, t2, mab, gc, x, wbr_c, wout, g_post)


def _ffn_chunks(ffn_dim):
    assert ffn_dim % LANES == 0
    edges = list(range(0, ffn_dim, FFN_CHUNK)) + [ffn_dim]
    return list(zip(edges[:-1], edges[1:]))


def _prompt_ffn_kernel(x_ref, gpre_ref, wup_ref, cw_ref, cb_ref, wdown_ref, gpost_ref,
                       o_ref, conv_ref, ext_ref, *, tiles_per_seq):
    tm = x_ref.shape[0]
    ffn = wdown_ref.shape[0]
    i = pl.program_id(0)

    @pl.when(i % tiles_per_seq == 0)
    def _():
        ext_ref[0:CONV_PREFIX, :] = jnp.zeros((CONV_PREFIX, ext_ref.shape[1]), F32)

    x = x_ref[...]
    h = _rms(x, gpre_ref[...]).astype(BF16)

    def conv(lo, hi):
        ext_ref[CONV_PREFIX:, lo:hi] = jnp.dot(h, wup_ref[:, lo:hi], preferred_element_type=F32)
        c = cb_ref[:, lo:hi]
        for j in range(CONV_WIDTH):
            s = CONV_PREFIX - (CONV_WIDTH - 1) + j
            c = c + ext_ref[s:s + tm, lo:hi] * cw_ref[j:j + 1, lo:hi]
        return c

    f = jnp.zeros((tm, o_ref.shape[1]), F32)
    for lo, hi in _ffn_chunks(ffn):
        act = _gelu_tanh(conv(lo, hi)) * conv(ffn + lo, ffn + hi)
        f = f + jnp.dot(act.astype(BF16), wdown_ref[lo:hi, :], preferred_element_type=F32)
    o_ref[...] = x + _rms(f, gpost_ref[...])

    last = ext_ref[tm:tm + CONV_PREFIX, :]
    conv_ref[0] = last
    ext_ref[0:CONV_PREFIX, :] = last


def _prompt_ffn(x2, gpre, wup, cw, cb, wdown, gpost, *, batch, seq):
    n, d = x2.shape
    tm = TOKEN_TILE
    tps = seq // tm
    f2 = wup.shape[1]
    return pl.pallas_call(
        functools.partial(_prompt_ffn_kernel, tiles_per_seq=tps),
        grid=(n // tm,),
        in_specs=[pl.BlockSpec((tm, d), lambda i: (i, 0)),
                  _const_spec(gpre.shape), _const_spec(wup.shape), _const_spec(cw.shape),
                  _const_spec(cb.shape), _const_spec(wdown.shape), _const_spec(gpost.shape)],
        out_specs=(pl.BlockSpec((tm, d), lambda i: (i, 0)),
                   pl.BlockSpec((1, CONV_PREFIX, f2), lambda i: (i // tps, 0, 0))),
        out_shape=(jax.ShapeDtypeStruct((n, d), F32),
                   jax.ShapeDtypeStruct((batch, CONV_PREFIX, f2), F32)),
        scratch_shapes=[pltpu.VMEM((tm + CONV_PREFIX, f2), F32)],
        compiler_params=pltpu.CompilerParams(dimension_semantics=("arbitrary",),
                                             vmem_limit_bytes=VMEM_LIMIT),
        name="prompt_ffn",
    )(x2, gpre, wup, cw, cb, wdown, gpost)


def _sample_in_kernel(x_ref, g_ref, win_ref, pre_ref, poolw_ref, pscale_ref, sgw0_ref, sgb0_ref,
                      wbr_ref, a_ref, sgv_ref, q_ref, k_ref, v_ref, mab_ref, gc_ref, *, pos0):
    h = _rms(x_ref[...], g_ref[...]).astype(BF16)

    def proj(lo, hi):
        return jnp.dot(h, win_ref[:, lo:hi], preferred_element_type=F32)

    a = proj(0, 512)
    a_ref[...] = a
    ya = []
    for gi, w in enumerate(POOL_WINDOWS):
        sl = slice(gi * LANES, (gi + 1) * LANES)
        s = a[:, sl]
        for j in range(1, w):
            s = s + pre_ref[POOL_BUF - j, :, sl]
        d = (s / float(min(pos0 + 1, w)) - a[:, sl]).astype(BF16)
        ya.append(jnp.dot(d, poolw_ref[gi], preferred_element_type=F32) * pscale_ref[:, sl])
    ya = jnp.concatenate(ya, axis=1).astype(BF16)

    u = proj(512, 1024)
    sv = proj(1024, 1536)
    sgv_ref[...] = sv
    yb = (u * (sgw0_ref[...] * sv + sgb0_ref[...])).astype(BF16)

    q_ref[...] = proj(1536, 2048) * SB_SCALE
    k_ref[...] = proj(2048, 2560)
    v_ref[...] = proj(2560, 3072)
    ga = jax.nn.sigmoid(proj(3072, 4096))
    gb = jax.nn.sigmoid(proj(4096, 5120))
    gc_ref[...] = jax.nn.sigmoid(proj(5120, 6144))
    mab_ref[...] = (ga * jnp.dot(ya, wbr_ref[0], preferred_element_type=F32)
                    + gb * jnp.dot(yb, wbr_ref[1], preferred_element_type=F32))


def _sample_in(x, g, win, pre, poolw, pscale, sgw0, sgb0, wbr_ab, *, pos0):
    n, d = x.shape
    sd = lambda w: jax.ShapeDtypeStruct((n, w), F32)
    return pl.pallas_call(
        functools.partial(_sample_in_kernel, pos0=pos0),
        out_shape=(sd(512), sd(512), sd(SB_WIDTH), sd(SB_WIDTH), sd(SB_WIDTH), sd(d), sd(d)),
        compiler_params=pltpu.CompilerParams(vmem_limit_bytes=VMEM_LIMIT),
        name="sample_in",
    )(x, g, win, pre, poolw, pscale, sgw0, sgb0, wbr_ab)


def _sample_attn_kernel(pt_ref, q_ref, bias_ref, t2_ref, *refs, n_pages):
    del pt_ref
    k_refs = refs[:n_pages]
    v_refs = refs[n_pages:2 * n_pages]
    o_ref = refs[2 * n_pages]

    hm = _head_mask((SB_HEADS, SB_WIDTH), 1, 0)
    qm = jnp.where(hm, jnp.broadcast_to(q_ref[0], (SB_HEADS, SB_WIDTH)), 0.0).astype(BF16)

    z, sp, cs, tot = [], [], [], []
    for p in range(n_pages):
        zp = lax.dot_general(qm, k_refs[p][0, 0].astype(BF16), (((1,), (1,)), ((), ())),
                             preferred_element_type=F32) + bias_ref[...]
        spp = _softplus(zp)
        r = jnp.dot(spp.astype(BF16), t2_ref[...], preferred_element_type=F32)
        z.append(zp)
        sp.append(spp)
        cs.append(r[:, :KEY_BLOCK])
        tot.append(r[:, KEY_BLOCK:])

    carry = jnp.zeros((SB_HEADS, KEY_BLOCK), F32)
    out = jnp.zeros((SB_HEADS, SB_WIDTH), F32)
    for p in reversed(range(n_pages)):
        a = jnp.exp(z[p] - sp[p] - cs[p] - carry)
        out = out + jnp.dot(a.astype(BF16), v_refs[p][0, 0].astype(BF16), preferred_element_type=F32)
        carry = carry + tot[p]
    o_ref[0] = jnp.sum(jnp.where(hm, out, 0.0), axis=0, keepdims=True)


def _sample_attn(page_table, q3, bias_col, t2, cache_k, cache_v, *, layer):
    n, n_pages = page_table.shape
    page = cache_k.shape[2]
    assert page == KEY_BLOCK and cache_k.shape[3] == SB_WIDTH

    def page_spec(p):
        return pl.BlockSpec((1, 1, page, SB_WIDTH), lambda b, pt: (layer, pt[b, p], 0, 0))

    kv_specs = [page_spec(p) for p in range(n_pages)]
    grid_spec = pltpu.PrefetchScalarGridSpec(
        num_scalar_prefetch=1,
        grid=(n,),
        in_specs=[pl.BlockSpec((1, 1, SB_WIDTH), lambda b, pt: (b, 0, 0)),
                  pl.BlockSpec(bias_col.shape, lambda b, pt: (0, 0)),
                  pl.BlockSpec(t2.shape, lambda b, pt: (0, 0))] + kv_specs + kv_specs,
        out_specs=pl.BlockSpec((1, 1, SB_WIDTH), lambda b, pt: (b, 0, 0)),
    )
    return pl.pallas_call(
        functools.partial(_sample_attn_kernel, n_pages=n_pages),
        grid_spec=grid_spec,
        out_shape=jax.ShapeDtypeStruct((n, 1, SB_WIDTH), F32),
        compiler_params=pltpu.CompilerParams(dimension_semantics=("arbitrary",),
                                             vmem_limit_bytes=VMEM_LIMIT),
        name="sample_attn",
    )(page_table, q3, bias_col, t2, *([cache_k] * n_pages), *([cache_v] * n_pages))


def _sample_out_kernel(yc_ref, mab_ref, gc_ref, x_ref, wbrc_ref, wout_ref, gpost_ref, gpre_ref,
                       wup_ref, cw_ref, cb_ref, pre_ref, wdown_ref, gfpost_ref, o_ref, up_ref):
    ffn = wdown_ref.shape[0]
    m = mab_ref[...] + gc_ref[...] * jnp.dot(yc_ref[...].astype(BF16), wbrc_ref[...],
                                             preferred_element_type=F32)
    mixed = jnp.dot(m.astype(BF16), wout_ref[...], preferred_element_type=F32)
    x = x_ref[...] + _rms(mixed, gpost_ref[...])
    h = _rms(x, gpre_ref[...]).astype(BF16)

    def conv(lo, hi):
        up = jnp.dot(h, wup_ref[:, lo:hi], preferred_element_type=F32)
        up_ref[:, lo:hi] = up
        c = cb_ref[:, lo:hi] + up * cw_ref[CONV_WIDTH - 1:CONV_WIDTH, lo:hi]
        for j in range(CONV_WIDTH - 1):
            c = c + pre_ref[j, :, lo:hi] * cw_ref[j:j + 1, lo:hi]
        return c

    f = jnp.zeros(x.shape, F32)
    for lo, hi in _ffn_chunks(ffn):
        act = _gelu_tanh(conv(lo, hi)) * conv(ffn + lo, ffn + hi)
        f = f + jnp.dot(act.astype(BF16), wdown_ref[lo:hi, :], preferred_element_type=F32)
    o_ref[...] = x + _rms(f, gfpost_ref[...])


def _sample_out(yc, mab, gc, x, wbr_c, wout, gpost, gpre, wup, cw, cb, pre, wdown, gfpost):
    n, d = x.shape
    return pl.pallas_call(
        _sample_out_kernel,
        out_shape=(jax.ShapeDtypeStruct((n, d), F32),
                   jax.ShapeDtypeStruct((n, wup.shape[1]), F32)),
        compiler_params=pltpu.CompilerParams(vmem_limit_bytes=VMEM_LIMIT),
        name="sample_out",
    )(yc, mab, gc, x, wbr_c, wout, gpost, gpre, wup, cw, cb, pre, wdown, gfpost)


def kernel(x_prompt, x_sample, state_pool, cache_k, cache_v, page_table, state_conv,
           norm_mix_pre, norm_mix_post, norm_ffn_pre, norm_ffn_post, w_in, pool_w,
           pool_scale, sg_w, sg_b, sb_bias, w_branch, w_out, w_up, conv_w, conv_b, w_down):
    depth = w_in.shape[0]
    batch, seq, d = x_prompt.shape
    dec = x_sample.shape[0]
    n_pool, page = cache_k.shape[1], cache_k.shape[2]
    past_len = page_table.shape[1] * page
    assert x_sample.shape[1] == 1 and past_len % CHUNK == 0
    assert seq % TOKEN_TILE == 0 and seq % QUERY_TILE == 0 and QUERY_TILE % KEY_BLOCK == 0

    t2 = _suffix_sum_matrix()
    ck = cache_k.reshape(depth, n_pool, page, SB_WIDTH)
    cv = cache_v.reshape(depth, n_pool, page, SB_WIDTH)
    pool_pre = jnp.transpose(state_pool, (0, 2, 1, 3))
    conv_pre = jnp.transpose(state_conv, (0, 2, 1, 3))

    xp = x_prompt.reshape(batch * seq, d)
    xs = x_sample.reshape(dec, d)
    outs = [[] for _ in range(9)]
    for l in range(depth):
        row = lambda w: w[l][None, :]
        win = w_in[l].astype(BF16)
        poolw = pool_w[l].astype(BF16)
        wbr = w_branch[l].astype(BF16)
        wout = w_out[l].astype(BF16)
        wup = w_up[l].astype(BF16)
        wdown = w_down[l].astype(BF16)
        pscale = row(pool_scale)
        cb = row(conv_b)
        g_pre, g_post = row(norm_mix_pre), row(norm_mix_post)
        gf_pre, gf_post = row(norm_ffn_pre), row(norm_ffn_post)

        q, k, v, kb, vb, mab, gc, pool = _prompt_in(
            xp, g_pre, win, poolw, pscale, sg_w[l], sg_b[l].T, wbr[:2], batch=batch, seq=seq)
        sh = lambda t: t.reshape(batch, seq, t.shape[-1])
        x1 = _prompt_attn(sb_bias[l], sh(q), sh(kb), sh(vb), t2, sh(mab), sh(gc), sh(xp),
                          wbr[2], wout, g_post)
        xp, conv = _prompt_ffn(x1.reshape(batch * seq, d), gf_pre, wup, conv_w[l], cb, wdown,
                               gf_post, batch=batch, seq=seq)

        sgw0 = jnp.repeat(sg_w[l][:, 0, 0], LANES)[None, :]
        sgb0 = jnp.repeat(sg_b[l][:, 0], LANES)[None, :]
        a_s, sgv, q_s, k_s, v_s, mab_s, gc_s = _sample_in(
            xs, g_pre, win, pool_pre[l], poolw, pscale, sgw0, sgb0, wbr[:2], pos0=past_len)
        yc_s = _sample_attn(page_table, q_s.reshape(dec, 1, SB_WIDTH), sb_bias[l][:, None], t2,
                            ck, cv, layer=l)
        xs, up_s = _sample_out(yc_s.reshape(dec, SB_WIDTH), mab_s, gc_s, xs, wbr[2], wout, g_post,
                               gf_pre, wup, conv_w[l], cb, conv_pre[l], wdown, gf_post)

        outs[0].append(pool[:, 1:])
        outs[1].append(jnp.concatenate([state_pool[l][:, 1:], a_s[:, None]], axis=1))
        outs[2].append(k.reshape(batch, seq, SB_HEADS, SB_HEAD_DIM))
        outs[3].append(v.reshape(batch, seq, SB_HEADS, SB_HEAD_DIM))
        outs[4].append(k_s.reshape(dec, 1, SB_HEADS, SB_HEAD_DIM))
        outs[5].append(v_s.reshape(dec, 1, SB_HEADS, SB_HEAD_DIM))
        outs[6].append(sgv[:, None])
        outs[7].append(conv[:, CONV_PREFIX - (CONV_WIDTH - 1):])
        outs[8].append(jnp.concatenate([state_conv[l][:, 1:], up_s[:, None]], axis=1))

    return (xp.reshape(batch, seq, d), xs.reshape(dec, 1, d)) + tuple(jnp.stack(o) for o in outs)
```

```python
import functools

import jax
import jax.numpy as jnp
from jax import lax
from jax.experimental import pallas as pl
from jax.experimental.pallas import tpu as pltpu

F32 = jnp.float32
BF16 = jnp.bfloat16

LANES = 128
POOL_WINDOWS = (2, 4, 8, 16)
POOL_PREFIX = max(POOL_WINDOWS)
POOL_BUF = POOL_PREFIX - 1
CHUNK = 128
SB_HEADS = 8
SB_HEAD_DIM = 64
SB_WIDTH = SB_HEADS * SB_HEAD_DIM
SB_SCALE = SB_HEAD_DIM ** -0.5
KEY_BLOCK = 128
CONV_WIDTH = 3
CONV_PREFIX = 8
EPS = 1e-6
LOG2E = 1.4426950408889634
VMEM_LIMIT = 56 * 1024 * 1024

TOKEN_TILE = 256
QUERY_TILE = 256
FFN_CHUNK = 256
SAMPLE_PAGES_PER_STEP = 8


def _rms(x, g):
    return x * lax.rsqrt(jnp.mean(x * x, axis=-1, keepdims=True) + EPS) * g


def _softplus(z):
    return jnp.maximum(z, 0.0) + jnp.log(1.0 + jnp.exp2(jnp.abs(z) * -LOG2E))


def _gelu_tanh(x):
    return 0.5 * x * (1.0 + jnp.tanh(0.7978845608028654 * (x + 0.044715 * (x * x * x))))


def _const_spec(shape):
    nd = len(shape)
    return pl.BlockSpec(shape, lambda *_: (0,) * nd, pipeline_mode=pl.Buffered(1))


def _suffix_sum_matrix():
    r = lax.broadcasted_iota(jnp.int32, (KEY_BLOCK, KEY_BLOCK), 0)
    c = lax.broadcasted_iota(jnp.int32, (KEY_BLOCK, KEY_BLOCK), 1)
    t = (r > c).astype(BF16)
    return jnp.concatenate([t, jnp.ones((KEY_BLOCK, KEY_BLOCK), BF16)], axis=1)


def _strict_lower(n):
    r = lax.broadcasted_iota(jnp.int32, (n, n), 0)
    c = lax.broadcasted_iota(jnp.int32, (n, n), 1)
    return (r > c).astype(BF16)


def _head_mask(shape, lane_axis, head_axis):
    lane = lax.broadcasted_iota(jnp.int32, shape, lane_axis)
    head = lax.broadcasted_iota(jnp.int32, shape, head_axis)
    return lane // SB_HEAD_DIM == head


def _prompt_in_kernel(x_ref, g_ref, win_ref, poolw_ref, pscale_ref, sgw_ref, sgb_ref, wbr_ref,
                      q_ref, k_ref, v_ref, kb_ref, vb_ref, mab_ref, gc_ref, pool_ref,
                      ext_ref, *, tiles_per_seq):
    tm = x_ref.shape[0]
    i = pl.program_id(0)
    t0 = (i % tiles_per_seq) * tm

    @pl.when(i % tiles_per_seq == 0)
    def _():
        ext_ref[0:POOL_PREFIX, :] = jnp.zeros((POOL_PREFIX, ext_ref.shape[1]), F32)

    h = _rms(x_ref[...], g_ref[...]).astype(BF16)

    def proj(lo, hi):
        return jnp.dot(h, win_ref[:, lo:hi], preferred_element_type=F32)

    ext_ref[POOL_PREFIX:, :] = proj(0, 512)
    pos1 = (t0 + 1 + lax.broadcasted_iota(jnp.int32, (tm, LANES), 0)).astype(F32)
    ya = []
    for gi, w in enumerate(POOL_WINDOWS):
        sl = slice(gi * LANES, (gi + 1) * LANES)
        cur = ext_ref[POOL_PREFIX:POOL_PREFIX + tm, sl]
        s = cur
        for j in range(1, w):
            s = s + ext_ref[POOL_PREFIX - j:POOL_PREFIX - j + tm, sl]
        d = (s / jnp.minimum(pos1, float(w)) - cur).astype(BF16)
        ya.append(jnp.dot(d, poolw_ref[gi], preferred_element_type=F32) * pscale_ref[:, sl])
    ya = jnp.concatenate(ya, axis=1).astype(BF16)
    last = ext_ref[tm:tm + POOL_PREFIX, :]
    pool_ref[0] = last
    ext_ref[0:POOL_PREFIX, :] = last

    u = proj(512, 1024)
    sv = proj(1024, 1536).astype(BF16)
    r = lax.broadcasted_iota(jnp.int32, (CHUNK, CHUNK), 0)
    c = lax.broadcasted_iota(jnp.int32, (CHUNK, CHUNK), 1)
    cols = []
    for gi in range(4):
        sl = slice(gi * LANES, (gi + 1) * LANES)
        wg = jnp.where(r >= c, sgw_ref[gi], 0.0).astype(BF16)
        bcol = sgb_ref[:, gi:gi + 1]
        rows = []
        for ci in range(tm // CHUNK):
            rs = slice(ci * CHUNK, (ci + 1) * CHUNK)
            mixed = jnp.dot(wg, sv[rs, sl], preferred_element_type=F32) + bcol
            rows.append(u[rs, sl] * mixed)
        cols.append(jnp.concatenate(rows, axis=0))
    yb = jnp.concatenate(cols, axis=1).astype(BF16)

    q_ref[...] = (proj(1536, 2048) * SB_SCALE).astype(BF16)
    k = proj(2048, 2560)
    k_ref[...] = k
    kb_ref[...] = k.astype(BF16)
    v = proj(2560, 3072)
    v_ref[...] = v
    vb_ref[...] = v.astype(BF16)

    ga = jax.nn.sigmoid(proj(3072, 4096))
    gb = jax.nn.sigmoid(proj(4096, 5120))
    gc_ref[...] = jax.nn.sigmoid(proj(5120, 6144)).astype(BF16)
    mab_ref[...] = (ga * jnp.dot(ya, wbr_ref[0], preferred_element_type=F32)
                    + gb * jnp.dot(yb, wbr_ref[1], preferred_element_type=F32))


def _prompt_in(x2, g, win, poolw, pscale, sgw, sgb_t, wbr_ab, *, batch, seq):
    n, d = x2.shape
    tm = TOKEN_TILE
    tps = seq // tm
    row = lambda w: pl.BlockSpec((tm, w), lambda i: (i, 0))
    out_shape = (
        jax.ShapeDtypeStruct((n, SB_WIDTH), BF16),
        jax.ShapeDtypeStruct((n, SB_WIDTH), F32),
        jax.ShapeDtypeStruct((n, SB_WIDTH), F32),
        jax.ShapeDtypeStruct((n, SB_WIDTH), BF16),
        jax.ShapeDtypeStruct((n, SB_WIDTH), BF16),
        jax.ShapeDtypeStruct((n, d), F32),
        jax.ShapeDtypeStruct((n, d), BF16),
        jax.ShapeDtypeStruct((batch, POOL_PREFIX, 512), F32),
    )
    out_specs = (row(SB_WIDTH), row(SB_WIDTH), row(SB_WIDTH), row(SB_WIDTH), row(SB_WIDTH),
                 row(d), row(d),
                 pl.BlockSpec((1, POOL_PREFIX, 512), lambda i: (i // tps, 0, 0)))
    return pl.pallas_call(
        functools.partial(_prompt_in_kernel, tiles_per_seq=tps),
        grid=(n // tm,),
        in_specs=[row(d), _const_spec(g.shape), _const_spec(win.shape), _const_spec(poolw.shape),
                  _const_spec(pscale.shape), _const_spec(sgw.shape), _const_spec(sgb_t.shape),
                  _const_spec(wbr_ab.shape)],
        out_specs=out_specs,
        out_shape=out_shape,
        scratch_shapes=[pltpu.VMEM((tm + POOL_PREFIX, 512), F32)],
        compiler_params=pltpu.CompilerParams(dimension_semantics=("arbitrary",),
                                             vmem_limit_bytes=VMEM_LIMIT),
        name="prompt_in",
    )(x2, g, win, poolw, pscale, sgw, sgb_t, wbr_ab)


def _prompt_attn_kernel(bias_ref, q_ref, kb_ref, vb_ref, tri_ref, mab_ref, gc_ref, x_ref,
                        wbrc_ref, wout_ref, g_ref, o_ref,
                        qm_ref, d_ref, spb_ref, s0_ref, ab_ref, acc_ref, carry_ref):
    tq = q_ref.shape[1]
    tk = tri_ref.shape[0]
    q0 = pl.program_id(1) * tq

    lane = lax.broadcasted_iota(jnp.int32, (tq, LANES), 1)
    low = lane < SB_HEAD_DIM
    for p in range(SB_HEADS // 2):
        qt = q_ref[0, :, p * LANES:(p + 1) * LANES]
        qm_ref[2 * p] = jnp.where(low, qt, 0.0).astype(BF16)
        qm_ref[2 * p + 1] = jnp.where(low, 0.0, qt).astype(BF16)
    acc_ref[...] = jnp.zeros(acc_ref.shape, F32)
    carry_ref[...] = jnp.zeros(carry_ref.shape, F32)

    def block(k0, mask):
        def tile(ref, h):
            return ref[0, pl.ds(k0, tk), (h // 2) * LANES:(h // 2 + 1) * LANES]

        for h in range(SB_HEADS):
            z = lax.dot_general(qm_ref[h], tile(kb_ref, h), (((1,), (1,)), ((), ())),
                                preferred_element_type=F32) + bias_ref[h]
            sp = _softplus(z)
            if mask is not None:
                sp = jnp.where(mask, sp, 0.0)
            d_ref[h] = z - sp
            spb_ref[h] = sp.astype(BF16)
            s0_ref[h] = sp[:, :LANES]

        def suffix(h):
            return jnp.dot(spb_ref[h], tri_ref[...], preferred_element_type=F32)

        def weights(h, c):
            cr = carry_ref[h]
            a = jnp.exp(d_ref[h] - c - jnp.concatenate([cr] * (tk // LANES), axis=1))
            if mask is not None:
                a = jnp.where(mask, a, 0.0)
            ab_ref[h] = a.astype(BF16)
            carry_ref[h] = cr + jnp.broadcast_to(c[:, 0:1] + s0_ref[h][:, 0:1], cr.shape)

        def accumulate(h):
            acc_ref[h] += jnp.dot(ab_ref[h], tile(vb_ref, h), preferred_element_type=F32)

        c_prev = suffix(0)
        for h in range(1, SB_HEADS):
            c = suffix(h)
            weights(h - 1, c_prev)
            accumulate(h - 1)
            c_prev = c
        weights(SB_HEADS - 1, c_prev)
        accumulate(SB_HEADS - 1)

    assert tq == tk
    block(pl.multiple_of(q0, tk), lax.broadcasted_iota(jnp.int32, (tq, tk), 1)
          < lax.broadcasted_iota(jnp.int32, (tq, tk), 0))

    def body(t, _):
        block(pl.multiple_of(q0 - (t + 1) * tk, tk), None)
        return 0

    lax.fori_loop(0, q0 // tk, body, 0)

    tiles = []
    for p in range(SB_HEADS // 2):
        tiles.append(jnp.where(low, acc_ref[2 * p], acc_ref[2 * p + 1]))
    yc = jnp.concatenate(tiles, axis=1).astype(BF16)
    m = mab_ref[0] + gc_ref[0].astype(F32) * jnp.dot(yc, wbrc_ref[...], preferred_element_type=F32)
    mixed = jnp.dot(m.astype(BF16), wout_ref[...], preferred_element_type=F32)
    o_ref[0] = x_ref[0] + _rms(mixed, g_ref[...])


def _prompt_attn(bias, q, kb, vb, tri, mab, gc, x, wbr_c, wout, g_post):
    b, s, d = x.shape
    tq = QUERY_TILE
    tk = tri.shape[0]
    row = lambda w: pl.BlockSpec((1, tq, w), lambda bi, qi: (bi, qi, 0))
    seq_spec = pl.BlockSpec((1, s, SB_WIDTH), lambda bi, qi: (bi, 0, 0), pipeline_mode=pl.Buffered(1))
    return pl.pallas_call(
        _prompt_attn_kernel,
        grid=(b, s // tq),
        in_specs=[pl.BlockSpec(memory_space=pltpu.SMEM),
                  row(SB_WIDTH), seq_spec, seq_spec, _const_spec(tri.shape),
                  row(d), row(d), row(d),
                  _const_spec(wbr_c.shape), _const_spec(wout.shape), _const_spec(g_post.shape)],
        out_specs=row(d),
        out_shape=jax.ShapeDtypeStruct((b, s, d), F32),
        scratch_shapes=[pltpu.VMEM((SB_HEADS, tq, LANES), BF16),
                        pltpu.VMEM((SB_HEADS, tq, tk), F32),
                        pltpu.VMEM((SB_HEADS, tq, tk), BF16),
                        pltpu.VMEM((SB_HEADS, tq, LANES), F32),
                        pltpu.VMEM((SB_HEADS, tq, tk), BF16),
                        pltpu.VMEM((SB_HEADS, tq, LANES), F32),
                        pltpu.VMEM((SB_HEADS, tq, LANES), F32)],
        compiler_params=pltpu.CompilerParams(dimension_semantics=("arbitrary", "arbitrary"),
                                             vmem_limit_bytes=VMEM_LIMIT),
        name="prompt_attn",
    )(bias, q, kb, vb, tri, mab, gc, x, wbr_c, wout, g_post)


def _ffn_chunks(ffn_dim):
    assert ffn_dim % LANES == 0
    edges = list(range(0, ffn_dim, FFN_CHUNK)) + [ffn_dim]
    return list(zip(edges[:-1], edges[1:]))


def _prompt_ffn_kernel(x_ref, gpre_ref, wup_ref, cw_ref, cb_ref, wdown_ref, gpost_ref,
                       o_ref, conv_ref, ext_ref, *, tiles_per_seq):
    tm = x_ref.shape[0]
    ffn = wdown_ref.shape[0]
    i = pl.program_id(0)

    @pl.when(i % tiles_per_seq == 0)
    def _():
        ext_ref[0:CONV_PREFIX, :] = jnp.zeros((CONV_PREFIX, ext_ref.shape[1]), F32)

    x = x_ref[...]
    h = _rms(x, gpre_ref[...]).astype(BF16)

    def conv(lo, hi):
        ext_ref[CONV_PREFIX:, lo:hi] = jnp.dot(h, wup_ref[:, lo:hi], preferred_element_type=F32)
        c = cb_ref[:, lo:hi]
        for j in range(CONV_WIDTH):
            s = CONV_PREFIX - (CONV_WIDTH - 1) + j
            c = c + ext_ref[s:s + tm, lo:hi] * cw_ref[j:j + 1, lo:hi]
        return c

    f = jnp.zeros((tm, o_ref.shape[1]), F32)
    for lo, hi in _ffn_chunks(ffn):
        act = _gelu_tanh(conv(lo, hi)) * conv(ffn + lo, ffn + hi)
        f = f + jnp.dot(act.astype(BF16), wdown_ref[lo:hi, :], preferred_element_type=F32)
    o_ref[...] = x + _rms(f, gpost_ref[...])

    last = ext_ref[tm:tm + CONV_PREFIX, :]
    conv_ref[0] = last
    ext_ref[0:CONV_PREFIX, :] = last


def _prompt_ffn(x2, gpre, wup, cw, cb, wdown, gpost, *, batch, seq):
    n, d = x2.shape
    tm = TOKEN_TILE
    tps = seq // tm
    f2 = wup.shape[1]
    return pl.pallas_call(
        functools.partial(_prompt_ffn_kernel, tiles_per_seq=tps),
        grid=(n // tm,),
        in_specs=[pl.BlockSpec((tm, d), lambda i: (i, 0)),
                  _const_spec(gpre.shape), _const_spec(wup.shape), _const_spec(cw.shape),
                  _const_spec(cb.shape), _const_spec(wdown.shape), _const_spec(gpost.shape)],
        out_specs=(pl.BlockSpec((tm, d), lambda i: (i, 0)),
                   pl.BlockSpec((1, CONV_PREFIX, f2), lambda i: (i // tps, 0, 0))),
        out_shape=(jax.ShapeDtypeStruct((n, d), F32),
                   jax.ShapeDtypeStruct((batch, CONV_PREFIX, f2), F32)),
        scratch_shapes=[pltpu.VMEM((tm + CONV_PREFIX, f2), F32)],
        compiler_params=pltpu.CompilerParams(dimension_semantics=("arbitrary",),
                                             vmem_limit_bytes=VMEM_LIMIT),
        name="prompt_ffn",
    )(x2, gpre, wup, cw, cb, wdown, gpost)


def _sample_in_kernel(x_ref, g_ref, win_ref, pre_ref, poolw_ref, pscale_ref, sgw0_ref, sgb0_ref,
                      wbr_ref, a_ref, sgv_ref, q_ref, k_ref, v_ref, mab_ref, gc_ref, *, pos0):
    h = _rms(x_ref[...], g_ref[...]).astype(BF16)

    def proj(lo, hi):
        return jnp.dot(h, win_ref[:, lo:hi], preferred_element_type=F32)

    a = proj(0, 512)
    a_ref[...] = a
    ya = []
    for gi, w in enumerate(POOL_WINDOWS):
        sl = slice(gi * LANES, (gi + 1) * LANES)
        s = a[:, sl]
        for j in range(1, w):
            s = s + pre_ref[POOL_BUF - j, :, sl]
        d = (s / float(min(pos0 + 1, w)) - a[:, sl]).astype(BF16)
        ya.append(jnp.dot(d, poolw_ref[gi], preferred_element_type=F32) * pscale_ref[:, sl])
    ya = jnp.concatenate(ya, axis=1).astype(BF16)

    u = proj(512, 1024)
    sv = proj(1024, 1536)
    sgv_ref[...] = sv
    yb = (u * (sgw0_ref[...] * sv + sgb0_ref[...])).astype(BF16)

    q_ref[...] = proj(1536, 2048) * SB_SCALE
    k_ref[...] = proj(2048, 2560)
    v_ref[...] = proj(2560, 3072)
    ga = jax.nn.sigmoid(proj(3072, 4096))
    gb = jax.nn.sigmoid(proj(4096, 5120))
    gc_ref[...] = jax.nn.sigmoid(proj(5120, 6144))
    mab_ref[...] = (ga * jnp.dot(ya, wbr_ref[0], preferred_element_type=F32)
                    + gb * jnp.dot(yb, wbr_ref[1], preferred_element_type=F32))


def _sample_in(x, g, win, pre, poolw, pscale, sgw0, sgb0, wbr_ab, *, pos0):
    n, d = x.shape
    sd = lambda w: jax.ShapeDtypeStruct((n, w), F32)
    return pl.pallas_call(
        functools.partial(_sample_in_kernel, pos0=pos0),
        out_shape=(sd(512), sd(512), sd(SB_WIDTH), sd(SB_WIDTH), sd(SB_WIDTH), sd(d), sd(d)),
        compiler_params=pltpu.CompilerParams(vmem_limit_bytes=VMEM_LIMIT),
        name="sample_in",
    )(x, g, win, pre, poolw, pscale, sgw0, sgb0, wbr_ab)


def _sample_attn_kernel(pt_ref, q_ref, bias_ref, t2_ref, *refs, pages_per_step):
    del pt_ref
    k_refs = refs[:pages_per_step]
    v_refs = refs[pages_per_step:2 * pages_per_step]
    o_ref, carry_ref = refs[2 * pages_per_step:]

    @pl.when(pl.program_id(1) == 0)
    def _():
        o_ref[...] = jnp.zeros(o_ref.shape, F32)
        carry_ref[...] = jnp.zeros(carry_ref.shape, F32)

    q8 = q_ref[0].astype(BF16)
    row_k = lax.broadcasted_iota(jnp.int32, (SB_HEADS, KEY_BLOCK), 0)
    row_d = lax.broadcasted_iota(jnp.int32, (SB_HEADS, SB_HEAD_DIM), 0)

    def head_rows(ref, h):
        return ref[0, 0, pl.ds(h, KEY_BLOCK, stride=SB_HEADS), :].astype(BF16)

    logits = []
    for i in range(pages_per_step):
        z = jnp.zeros((SB_HEADS, KEY_BLOCK), F32)
        for h in range(SB_HEADS):
            res = lax.dot_general(q8, head_rows(k_refs[i], h), (((1,), (1,)), ((), ())),
                                  preferred_element_type=F32)
            z = jnp.where(row_k == h, res, z)
        logits.append(z + bias_ref[...])
    softplus = [_softplus(z) for z in logits]
    sums = [jnp.dot(sp.astype(BF16), t2_ref[...], preferred_element_type=F32) for sp in softplus]
    carry = carry_ref[...]
    weights = []
    for z, sp, r in zip(logits, softplus, sums):
        weights.append(jnp.exp(z - sp - r[:, :KEY_BLOCK] - carry).astype(BF16))
        carry = carry + r[:, KEY_BLOCK:]
    carry_ref[...] = carry
    out = o_ref[0]
    for i in range(pages_per_step):
        for h in range(SB_HEADS):
            res = jnp.dot(weights[i], head_rows(v_refs[i], h), preferred_element_type=F32)
            out = out + jnp.where(row_d == h, res, 0.0)
    o_ref[0] = out


def _sample_attn(page_table, q8, bias_col, t2, cache_k, cache_v, *, layer):
    n, n_pages = page_table.shape
    depth, n_pool, page = cache_k.shape[:3]
    assert page == KEY_BLOCK and cache_k.shape[3:] == (SB_HEADS, SB_HEAD_DIM)
    pps = SAMPLE_PAGES_PER_STEP
    assert n_pages % pps == 0
    rows = page * SB_HEADS
    ck = cache_k.reshape(depth, n_pool, rows, SB_HEAD_DIM)
    cv = cache_v.reshape(depth, n_pool, rows, SB_HEAD_DIM)

    def page_spec(i):
        return pl.BlockSpec((1, 1, rows, SB_HEAD_DIM),
                            lambda b, s, pt: (layer, pt[b, n_pages - 1 - (s * pps + i)], 0, 0))

    kv_specs = [page_spec(i) for i in range(pps)]
    head_spec = pl.BlockSpec((1, SB_HEADS, SB_HEAD_DIM), lambda b, s, pt: (b, 0, 0))
    grid_spec = pltpu.PrefetchScalarGridSpec(
        num_scalar_prefetch=1,
        grid=(n, n_pages // pps),
        in_specs=[head_spec,
                  pl.BlockSpec(bias_col.shape, lambda b, s, pt: (0, 0)),
                  pl.BlockSpec(t2.shape, lambda b, s, pt: (0, 0))] + kv_specs + kv_specs,
        out_specs=head_spec,
        scratch_shapes=[pltpu.VMEM((SB_HEADS, KEY_BLOCK), F32)],
    )
    return pl.pallas_call(
        functools.partial(_sample_attn_kernel, pages_per_step=pps),
        grid_spec=grid_spec,
        out_shape=jax.ShapeDtypeStruct((n, SB_HEADS, SB_HEAD_DIM), F32),
        compiler_params=pltpu.CompilerParams(dimension_semantics=("arbitrary", "arbitrary"),
                                             vmem_limit_bytes=VMEM_LIMIT),
        name="sample_attn",
    )(page_table, q8, bias_col, t2, *([ck] * pps), *([cv] * pps))


def _sample_out_kernel(yc_ref, mab_ref, gc_ref, x_ref, wbrc_ref, wout_ref, gpost_ref, gpre_ref,
                       wup_ref, cw_ref, cb_ref, pre_ref, wdown_ref, gfpost_ref, o_ref, up_ref):
    ffn = wdown_ref.shape[0]
    m = mab_ref[...] + gc_ref[...] * jnp.dot(yc_ref[...].astype(BF16), wbrc_ref[...],
                                             preferred_element_type=F32)
    mixed = jnp.dot(m.astype(BF16), wout_ref[...], preferred_element_type=F32)
    x = x_ref[...] + _rms(mixed, gpost_ref[...])
    h = _rms(x, gpre_ref[...]).astype(BF16)

    def conv(lo, hi):
        up = jnp.dot(h, wup_ref[:, lo:hi], preferred_element_type=F32)
        up_ref[:, lo:hi] = up
        c = cb_ref[:, lo:hi] + up * cw_ref[CONV_WIDTH - 1:CONV_WIDTH, lo:hi]
        for j in range(CONV_WIDTH - 1):
            c = c + pre_ref[j, :, lo:hi] * cw_ref[j:j + 1, lo:hi]
        return c

    f = jnp.zeros(x.shape, F32)
    for lo, hi in _ffn_chunks(ffn):
        act = _gelu_tanh(conv(lo, hi)) * conv(ffn + lo, ffn + hi)
        f = f + jnp.dot(act.astype(BF16), wdown_ref[lo:hi, :], preferred_element_type=F32)
    o_ref[...] = x + _rms(f, gfpost_ref[...])


def _sample_out(yc, mab, gc, x, wbr_c, wout, gpost, gpre, wup, cw, cb, pre, wdown, gfpost):
    n, d = x.shape
    return pl.pallas_call(
        _sample_out_kernel,
        out_shape=(jax.ShapeDtypeStruct((n, d), F32),
                   jax.ShapeDtypeStruct((n, wup.shape[1]), F32)),
        compiler_params=pltpu.CompilerParams(vmem_limit_bytes=VMEM_LIMIT),
        name="sample_out",
    )(yc, mab, gc, x, wbr_c, wout, gpost, gpre, wup, cw, cb, pre, wdown, gfpost)


def kernel(x_prompt, x_sample, state_pool, cache_k, cache_v, page_table, state_conv,
           norm_mix_pre, norm_mix_post, norm_ffn_pre, norm_ffn_post, w_in, pool_w,
           pool_scale, sg_w, sg_b, sb_bias, w_branch, w_out, w_up, conv_w, conv_b, w_down):
    depth = w_in.shape[0]
    batch, seq, d = x_prompt.shape
    dec = x_sample.shape[0]
    n_pool, page = cache_k.shape[1], cache_k.shape[2]
    past_len = page_table.shape[1] * page
    assert x_sample.shape[1] == 1 and past_len % CHUNK == 0
    assert seq % TOKEN_TILE == 0 and seq % QUERY_TILE == 0

    t2 = _suffix_sum_matrix()
    tri = _strict_lower(QUERY_TILE)
    pool_pre = jnp.transpose(state_pool, (0, 2, 1, 3))
    conv_pre = jnp.transpose(state_conv, (0, 2, 1, 3))

    xp = x_prompt.reshape(batch * seq, d)
    xs = x_sample.reshape(dec, d)
    outs = [[] for _ in range(9)]
    for l in range(depth):
        row = lambda w: w[l][None, :]
        win = w_in[l].astype(BF16)
        poolw = pool_w[l].astype(BF16)
        wbr = w_branch[l].astype(BF16)
        wout = w_out[l].astype(BF16)
        wup = w_up[l].astype(BF16)
        wdown = w_down[l].astype(BF16)
        pscale = row(pool_scale)
        cb = row(conv_b)
        g_pre, g_post = row(norm_mix_pre), row(norm_mix_post)
        gf_pre, gf_post = row(norm_ffn_pre), row(norm_ffn_post)

        q, k, v, kb, vb, mab, gc, pool = _prompt_in(
            xp, g_pre, win, poolw, pscale, sg_w[l], sg_b[l].T, wbr[:2], batch=batch, seq=seq)
        sh = lambda t: t.reshape(batch, seq, t.shape[-1])
        x1 = _prompt_attn(sb_bias[l], sh(q), sh(kb), sh(vb), tri, sh(mab), sh(gc), sh(xp),
                          wbr[2], wout, g_post)
        xp, conv = _prompt_ffn(x1.reshape(batch * seq, d), gf_pre, wup, conv_w[l], cb, wdown,
                               gf_post, batch=batch, seq=seq)

        sgw0 = jnp.repeat(sg_w[l][:, 0, 0], LANES)[None, :]
        sgb0 = jnp.repeat(sg_b[l][:, 0], LANES)[None, :]
        a_s, sgv, q_s, k_s, v_s, mab_s, gc_s = _sample_in(
            xs, g_pre, win, pool_pre[l], poolw, pscale, sgw0, sgb0, wbr[:2], pos0=past_len)
        yc_s = _sample_attn(page_table, q_s.reshape(dec, SB_HEADS, SB_HEAD_DIM), sb_bias[l][:, None],
                            t2, cache_k, cache_v, layer=l)
        xs, up_s = _sample_out(yc_s.reshape(dec, SB_WIDTH), mab_s, gc_s, xs, wbr[2], wout, g_post,
                               gf_pre, wup, conv_w[l], cb, conv_pre[l], wdown, gf_post)

        outs[0].append(pool[:, 1:])
        outs[1].append(jnp.concatenate([state_pool[l][:, 1:], a_s[:, None]], axis=1))
        outs[2].append(k.reshape(batch, seq, SB_HEADS, SB_HEAD_DIM))
        outs[3].append(v.reshape(batch, seq, SB_HEADS, SB_HEAD_DIM))
        outs[4].append(k_s.reshape(dec, 1, SB_HEADS, SB_HEAD_DIM))
        outs[5].append(v_s.reshape(dec, 1, SB_HEADS, SB_HEAD_DIM))
        outs[6].append(sgv[:, None])
        outs[7].append(conv[:, CONV_PREFIX - (CONV_WIDTH - 1):])
        outs[8].append(jnp.concatenate([state_conv[l][:, 1:], up_s[:, None]], axis=1))

    return (xp.reshape(batch, seq, d), xs.reshape(dec, 1, d)) + tuple(jnp.stack(o) for o in outs)
```

```python
import functools

import jax
import jax.numpy as jnp
from jax import lax
from jax.experimental import pallas as pl
from jax.experimental.pallas import tpu as pltpu

F32 = jnp.float32
BF16 = jnp.bfloat16

LANES = 128
POOL_WINDOWS = (2, 4, 8, 16)
POOL_PREFIX = max(POOL_WINDOWS)
POOL_BUF = POOL_PREFIX - 1
CHUNK = 128
SB_HEADS = 8
SB_HEAD_DIM = 64
SB_WIDTH = SB_HEADS * SB_HEAD_DIM
SB_SCALE = SB_HEAD_DIM ** -0.5
KEY_BLOCK = 128
CONV_WIDTH = 3
CONV_PREFIX = 8
EPS = 1e-6
LOG2E = 1.4426950408889634
VMEM_LIMIT = 56 * 1024 * 1024

TOKEN_TILE = 256
FFN_TOKEN_TILE = 256
QUERY_TILE = 256
FFN_CHUNK = 256
FFN_LOOKAHEAD = 2
SAMPLE_PAGES_PER_STEP = 8


def _rms(x, g):
    return x * lax.rsqrt(jnp.mean(x * x, axis=-1, keepdims=True) + EPS) * g


def _softplus(z):
    return jnp.maximum(z, 0.0) + jnp.log(1.0 + jnp.exp2(jnp.abs(z) * -LOG2E))


def _gelu_tanh(x):
    return 0.5 * x * (1.0 + jnp.tanh(0.7978845608028654 * (x + 0.044715 * (x * x * x))))


def _const_spec(shape):
    nd = len(shape)
    return pl.BlockSpec(shape, lambda *_: (0,) * nd, pipeline_mode=pl.Buffered(1))


def _suffix_sum_matrix(n):
    return jnp.concatenate([_strict_lower(n), jnp.ones((n, n), BF16)], axis=1)


def _strict_lower(n):
    r = lax.broadcasted_iota(jnp.int32, (n, n), 0)
    c = lax.broadcasted_iota(jnp.int32, (n, n), 1)
    return (r > c).astype(BF16)


def _prompt_in_kernel(x_ref, g_ref, win_ref, poolw_ref, pscale_ref, sgw_ref, sgb_ref, wbr_ref,
                      q_ref, k_ref, v_ref, kb_ref, vb_ref, mab_ref, gc_ref, pool_ref,
                      ext_ref, *, tiles_per_seq):
    tm = x_ref.shape[0]
    i = pl.program_id(0)
    t0 = (i % tiles_per_seq) * tm

    @pl.when(i % tiles_per_seq == 0)
    def _():
        ext_ref[0:POOL_PREFIX, :] = jnp.zeros((POOL_PREFIX, ext_ref.shape[1]), F32)

    h = _rms(x_ref[...], g_ref[...]).astype(BF16)

    def proj(lo, hi):
        return jnp.dot(h, win_ref[:, lo:hi], preferred_element_type=F32)

    ext_ref[POOL_PREFIX:, :] = proj(0, 512)
    pos1 = (t0 + 1 + lax.broadcasted_iota(jnp.int32, (tm, LANES), 0)).astype(F32)
    ya = []
    for gi, w in enumerate(POOL_WINDOWS):
        sl = slice(gi * LANES, (gi + 1) * LANES)
        cur = ext_ref[POOL_PREFIX:POOL_PREFIX + tm, sl]
        s = cur
        for j in range(1, w):
            s = s + ext_ref[POOL_PREFIX - j:POOL_PREFIX - j + tm, sl]
        d = (s / jnp.minimum(pos1, float(w)) - cur).astype(BF16)
        ya.append(jnp.dot(d, poolw_ref[gi], preferred_element_type=F32) * pscale_ref[:, sl])
    ya = jnp.concatenate(ya, axis=1).astype(BF16)
    last = ext_ref[tm:tm + POOL_PREFIX, :]
    pool_ref[0] = last
    ext_ref[0:POOL_PREFIX, :] = last

    u = proj(512, 1024)
    sv = proj(1024, 1536).astype(BF16)
    r = lax.broadcasted_iota(jnp.int32, (CHUNK, CHUNK), 0)
    c = lax.broadcasted_iota(jnp.int32, (CHUNK, CHUNK), 1)
    cols = []
    for gi in range(4):
        sl = slice(gi * LANES, (gi + 1) * LANES)
        wg = jnp.where(r >= c, sgw_ref[gi], 0.0).astype(BF16)
        bcol = sgb_ref[:, gi:gi + 1]
        rows = []
        for ci in range(tm // CHUNK):
            rs = slice(ci * CHUNK, (ci + 1) * CHUNK)
            mixed = jnp.dot(wg, sv[rs, sl], preferred_element_type=F32) + bcol
            rows.append(u[rs, sl] * mixed)
        cols.append(jnp.concatenate(rows, axis=0))
    yb = jnp.concatenate(cols, axis=1).astype(BF16)

    q_ref[...] = (proj(1536, 2048) * SB_SCALE).astype(BF16)
    k = proj(2048, 2560)
    k_ref[...] = k
    kb_ref[...] = k.astype(BF16)
    v = proj(2560, 3072)
    v_ref[...] = v
    vb_ref[...] = v.astype(BF16)

    ga = jax.nn.sigmoid(proj(3072, 4096))
    gb = jax.nn.sigmoid(proj(4096, 5120))
    gc_ref[...] = jax.nn.sigmoid(proj(5120, 6144)).astype(BF16)
    mab_ref[...] = (ga * jnp.dot(ya, wbr_ref[0], preferred_element_type=F32)
                    + gb * jnp.dot(yb, wbr_ref[1], preferred_element_type=F32))


def _prompt_in(x2, g, win, poolw, pscale, sgw, sgb_t, wbr_ab, *, batch, seq):
    n, d = x2.shape
    tm = TOKEN_TILE
    tps = seq // tm
    row = lambda w: pl.BlockSpec((tm, w), lambda i: (i, 0))
    out_shape = (
        jax.ShapeDtypeStruct((n, SB_WIDTH), BF16),
        jax.ShapeDtypeStruct((n, SB_WIDTH), F32),
        jax.ShapeDtypeStruct((n, SB_WIDTH), F32),
        jax.ShapeDtypeStruct((n, SB_WIDTH), BF16),
        jax.ShapeDtypeStruct((n, SB_WIDTH), BF16),
        jax.ShapeDtypeStruct((n, d), F32),
        jax.ShapeDtypeStruct((n, d), BF16),
        jax.ShapeDtypeStruct((batch, POOL_PREFIX, 512), F32),
    )
    out_specs = (row(SB_WIDTH), row(SB_WIDTH), row(SB_WIDTH), row(SB_WIDTH), row(SB_WIDTH),
                 row(d), row(d),
                 pl.BlockSpec((1, POOL_PREFIX, 512), lambda i: (i // tps, 0, 0)))
    return pl.pallas_call(
        functools.partial(_prompt_in_kernel, tiles_per_seq=tps),
        grid=(n // tm,),
        in_specs=[row(d), _const_spec(g.shape), _const_spec(win.shape), _const_spec(poolw.shape),
                  _const_spec(pscale.shape), _const_spec(sgw.shape), _const_spec(sgb_t.shape),
                  _const_spec(wbr_ab.shape)],
        out_specs=out_specs,
        out_shape=out_shape,
        scratch_shapes=[pltpu.VMEM((tm + POOL_PREFIX, 512), F32)],
        compiler_params=pltpu.CompilerParams(dimension_semantics=("arbitrary",),
                                             vmem_limit_bytes=VMEM_LIMIT),
        name="prompt_in",
    )(x2, g, win, poolw, pscale, sgw, sgb_t, wbr_ab)


def _prompt_attn_kernel(bias_ref, q_ref, kb_ref, vb_ref, tri_ref, mab_ref, gc_ref, x_ref,
                        wbrc_ref, wout_ref, g_ref, o_ref,
                        qm_ref, d_ref, spb_ref, s0_ref, ab_ref, acc_ref, carry_ref):
    tq = q_ref.shape[1]
    tk = tri_ref.shape[0]
    q0 = pl.program_id(1) * tq

    lane = lax.broadcasted_iota(jnp.int32, (tq, LANES), 1)
    low = lane < SB_HEAD_DIM
    for p in range(SB_HEADS // 2):
        qt = q_ref[0, :, p * LANES:(p + 1) * LANES]
        qm_ref[2 * p] = jnp.where(low, qt, 0.0).astype(BF16)
        qm_ref[2 * p + 1] = jnp.where(low, 0.0, qt).astype(BF16)
    acc_ref[...] = jnp.zeros(acc_ref.shape, F32)
    carry_ref[...] = jnp.zeros(carry_ref.shape, F32)

    def block(k0, mask):
        def tile(ref, h):
            return ref[0, pl.ds(k0, tk), (h // 2) * LANES:(h // 2 + 1) * LANES]

        for h in range(SB_HEADS):
            z = lax.dot_general(qm_ref[h], tile(kb_ref, h), (((1,), (1,)), ((), ())),
                                preferred_element_type=F32) + bias_ref[h]
            sp = _softplus(z)
            if mask is not None:
                sp = jnp.where(mask, sp, 0.0)
            d_ref[h] = z - sp
            spb_ref[h] = sp.astype(BF16)
            s0_ref[h] = sp[:, :LANES]

        def suffix(h):
            return jnp.dot(spb_ref[h], tri_ref[...], preferred_element_type=F32)

        def weights(h, c):
            cr = carry_ref[h]
            a = jnp.exp(d_ref[h] - c - jnp.concatenate([cr] * (tk // LANES), axis=1))
            if mask is not None:
                a = jnp.where(mask, a, 0.0)
            ab_ref[h] = a.astype(BF16)
            carry_ref[h] = cr + jnp.broadcast_to(c[:, 0:1] + s0_ref[h][:, 0:1], cr.shape)

        def accumulate(h):
            acc_ref[h] += jnp.dot(ab_ref[h], tile(vb_ref, h), preferred_element_type=F32)

        c_prev = suffix(0)
        for h in range(1, SB_HEADS):
            c = suffix(h)
            weights(h - 1, c_prev)
            accumulate(h - 1)
            c_prev = c
        weights(SB_HEADS - 1, c_prev)
        accumulate(SB_HEADS - 1)

    assert tq == tk
    block(pl.multiple_of(q0, tk), lax.broadcasted_iota(jnp.int32, (tq, tk), 1)
          < lax.broadcasted_iota(jnp.int32, (tq, tk), 0))

    def body(t, _):
        block(pl.multiple_of(q0 - (t + 1) * tk, tk), None)
        return 0

    lax.fori_loop(0, q0 // tk, body, 0)

    tiles = []
    for p in range(SB_HEADS // 2):
        tiles.append(jnp.where(low, acc_ref[2 * p], acc_ref[2 * p + 1]))
    yc = jnp.concatenate(tiles, axis=1).astype(BF16)
    m = mab_ref[0] + gc_ref[0].astype(F32) * jnp.dot(yc, wbrc_ref[...], preferred_element_type=F32)
    mixed = jnp.dot(m.astype(BF16), wout_ref[...], preferred_element_type=F32)
    o_ref[0] = x_ref[0] + _rms(mixed, g_ref[...])


def _prompt_attn(bias, q, kb, vb, tri, mab, gc, x, wbr_c, wout, g_post):
    b, s, d = x.shape
    tq = QUERY_TILE
    tk = tri.shape[0]
    row = lambda w: pl.BlockSpec((1, tq, w), lambda bi, qi: (bi, qi, 0))
    seq_spec = pl.BlockSpec((1, s, SB_WIDTH), lambda bi, qi: (bi, 0, 0), pipeline_mode=pl.Buffered(1))
    return pl.pallas_call(
        _prompt_attn_kernel,
        grid=(b, s // tq),
        in_specs=[pl.BlockSpec(memory_space=pltpu.SMEM),
                  row(SB_WIDTH), seq_spec, seq_spec, _const_spec(tri.shape),
                  row(d), row(d), row(d),
                  _const_spec(wbr_c.shape), _const_spec(wout.shape), _const_spec(g_post.shape)],
        out_specs=row(d),
        out_shape=jax.ShapeDtypeStruct((b, s, d), F32),
        scratch_shapes=[pltpu.VMEM((SB_HEADS, tq, LANES), BF16),
                        pltpu.VMEM((SB_HEADS, tq, tk), F32),
                        pltpu.VMEM((SB_HEADS, tq, tk), BF16),
                        pltpu.VMEM((SB_HEADS, tq, LANES), F32),
                        pltpu.VMEM((SB_HEADS, tq, tk), BF16),
                        pltpu.VMEM((SB_HEADS, tq, LANES), F32),
                        pltpu.VMEM((SB_HEADS, tq, LANES), F32)],
        compiler_params=pltpu.CompilerParams(dimension_semantics=("arbitrary", "arbitrary"),
                                             vmem_limit_bytes=VMEM_LIMIT),
        name="prompt_attn",
    )(bias, q, kb, vb, tri, mab, gc, x, wbr_c, wout, g_post)


def _ffn_chunks(ffn_dim):
    assert ffn_dim % LANES == 0
    edges = list(range(0, ffn_dim, FFN_CHUNK)) + [ffn_dim]
    return list(zip(edges[:-1], edges[1:]))


def _prompt_ffn_kernel(x_ref, gpre_ref, wup_ref, cw_ref, cb_ref, wdown_ref, gpost_ref,
                       o_ref, conv_ref, ext_ref, *, tiles_per_seq):
    tm = x_ref.shape[0]
    ffn = wdown_ref.shape[0]
    i = pl.program_id(0)

    @pl.when(i % tiles_per_seq == 0)
    def _():
        ext_ref[0:CONV_PREFIX, :] = jnp.zeros((CONV_PREFIX, ext_ref.shape[1]), F32)

    x = x_ref[...]
    h = _rms(x, gpre_ref[...]).astype(BF16)

    def up(lo, hi):
        ext_ref[CONV_PREFIX:, lo:hi] = jnp.dot(h, wup_ref[:, lo:hi], preferred_element_type=F32)

    def conv(lo, hi):
        c = cb_ref[:, lo:hi]
        for j in range(CONV_WIDTH):
            s = CONV_PREFIX - (CONV_WIDTH - 1) + j
            c = c + ext_ref[s:s + tm, lo:hi] * cw_ref[j:j + 1, lo:hi]
        return c

    chunks = _ffn_chunks(ffn)
    f = jnp.zeros((tm, o_ref.shape[1]), F32)
    pending = None
    for i in range(len(chunks) + FFN_LOOKAHEAD + 1):
        if i < len(chunks):
            lo, hi = chunks[i]
            up(lo, hi)
            up(ffn + lo, ffn + hi)
        if pending is not None:
            act, lo, hi = pending
            f = f + jnp.dot(act, wdown_ref[lo:hi, :], preferred_element_type=F32)
            pending = None
        if FFN_LOOKAHEAD <= i < len(chunks) + FFN_LOOKAHEAD:
            lo, hi = chunks[i - FFN_LOOKAHEAD]
            act = _gelu_tanh(conv(lo, hi)) * conv(ffn + lo, ffn + hi)
            pending = (act.astype(BF16), lo, hi)
    o_ref[...] = x + _rms(f, gpost_ref[...])

    last = ext_ref[tm:tm + CONV_PREFIX, :]
    conv_ref[0] = last
    ext_ref[0:CONV_PREFIX, :] = last


def _prompt_ffn(x2, gpre, wup, cw, cb, wdown, gpost, *, batch, seq):
    n, d = x2.shape
    tm = FFN_TOKEN_TILE
    tps = seq // tm
    f2 = wup.shape[1]
    return pl.pallas_call(
        functools.partial(_prompt_ffn_kernel, tiles_per_seq=tps),
        grid=(n // tm,),
        in_specs=[pl.BlockSpec((tm, d), lambda i: (i, 0)),
                  _const_spec(gpre.shape), _const_spec(wup.shape), _const_spec(cw.shape),
                  _const_spec(cb.shape), _const_spec(wdown.shape), _const_spec(gpost.shape)],
        out_specs=(pl.BlockSpec((tm, d), lambda i: (i, 0)),
                   pl.BlockSpec((1, CONV_PREFIX, f2), lambda i: (i // tps, 0, 0))),
        out_shape=(jax.ShapeDtypeStruct((n, d), F32),
                   jax.ShapeDtypeStruct((batch, CONV_PREFIX, f2), F32)),
        scratch_shapes=[pltpu.VMEM((tm + CONV_PREFIX, f2), F32)],
        compiler_params=pltpu.CompilerParams(dimension_semantics=("arbitrary",),
                                             vmem_limit_bytes=VMEM_LIMIT),
        name="prompt_ffn",
    )(x2, gpre, wup, cw, cb, wdown, gpost)


def _sample_in_kernel(x_ref, g_ref, win_ref, pre_ref, poolw_ref, pscale_ref, sgw0_ref, sgb0_ref,
                      wbr_ref, a_ref, sgv_ref, q_ref, k_ref, v_ref, mab_ref, gc_ref, *, pos0):
    h = _rms(x_ref[...], g_ref[...]).astype(BF16)

    def proj(lo, hi):
        return jnp.dot(h, win_ref[:, lo:hi], preferred_element_type=F32)

    a = proj(0, 512)
    a_ref[...] = a
    ya = []
    for gi, w in enumerate(POOL_WINDOWS):
        sl = slice(gi * LANES, (gi + 1) * LANES)
        s = a[:, sl]
        for j in range(1, w):
            s = s + pre_ref[POOL_BUF - j, :, sl]
        d = (s / float(min(pos0 + 1, w)) - a[:, sl]).astype(BF16)
        ya.append(jnp.dot(d, poolw_ref[gi], preferred_element_type=F32) * pscale_ref[:, sl])
    ya = jnp.concatenate(ya, axis=1).astype(BF16)

    u = proj(512, 1024)
    sv = proj(1024, 1536)
    sgv_ref[...] = sv
    yb = (u * (sgw0_ref[...] * sv + sgb0_ref[...])).astype(BF16)

    q_ref[...] = proj(1536, 2048) * SB_SCALE
    k_ref[...] = proj(2048, 2560)
    v_ref[...] = proj(2560, 3072)
    ga = jax.nn.sigmoid(proj(3072, 4096))
    gb = jax.nn.sigmoid(proj(4096, 5120))
    gc_ref[...] = jax.nn.sigmoid(proj(5120, 6144))
    mab_ref[...] = (ga * jnp.dot(ya, wbr_ref[0], preferred_element_type=F32)
                    + gb * jnp.dot(yb, wbr_ref[1], preferred_element_type=F32))


def _sample_in(x, g, win, pre, poolw, pscale, sgw0, sgb0, wbr_ab, *, pos0):
    n, d = x.shape
    sd = lambda w: jax.ShapeDtypeStruct((n, w), F32)
    return pl.pallas_call(
        functools.partial(_sample_in_kernel, pos0=pos0),
        out_shape=(sd(512), sd(512), sd(SB_WIDTH), sd(SB_WIDTH), sd(SB_WIDTH), sd(d), sd(d)),
        compiler_params=pltpu.CompilerParams(vmem_limit_bytes=VMEM_LIMIT),
        name="sample_in",
    )(x, g, win, pre, poolw, pscale, sgw0, sgb0, wbr_ab)


def _sample_attn_kernel(pt_ref, q_ref, bias_ref, t2_ref, *refs, pages_per_step):
    del pt_ref
    k_refs = refs[:pages_per_step]
    v_refs = refs[pages_per_step:2 * pages_per_step]
    o_ref, carry_ref = refs[2 * pages_per_step:]

    @pl.when(pl.program_id(1) == 0)
    def _():
        o_ref[...] = jnp.zeros(o_ref.shape, F32)
        carry_ref[...] = jnp.zeros(carry_ref.shape, F32)

    q8 = q_ref[0].astype(BF16)
    row_k = lax.broadcasted_iota(jnp.int32, (SB_HEADS, KEY_BLOCK), 0)
    row_d = lax.broadcasted_iota(jnp.int32, (SB_HEADS, SB_HEAD_DIM), 0)

    logits = []
    for i in range(pages_per_step):
        z = jnp.zeros((SB_HEADS, KEY_BLOCK), F32)
        for h in range(SB_HEADS):
            res = jnp.dot(q8, k_refs[i][h].astype(BF16), preferred_element_type=F32)
            z = jnp.where(row_k == h, res, z)
        logits.append(z + bias_ref[...])
    softplus = [_softplus(z) for z in logits]
    sums = [jnp.dot(sp.astype(BF16), t2_ref[...], preferred_element_type=F32) for sp in softplus]
    carry = carry_ref[...]
    weights = []
    for z, sp, r in zip(logits, softplus, sums):
        weights.append(jnp.exp(z - sp - r[:, :KEY_BLOCK] - carry).astype(BF16))
        carry = carry + r[:, KEY_BLOCK:]
    carry_ref[...] = carry
    out = o_ref[0]
    for i in range(pages_per_step):
        for h in range(SB_HEADS):
            res = lax.dot_general(weights[i], v_refs[i][h].astype(BF16), (((1,), (1,)), ((), ())),
                                  preferred_element_type=F32)
            out = out + jnp.where(row_d == h, res, 0.0)
    o_ref[0] = out


def _keys_minor(cache):
    return jnp.transpose(cache, (0, 1, 3, 4, 2))


def _sample_attn(page_table, q, bias_col, t2, cache_kt, cache_vt, *, layer):
    n, n_pages = page_table.shape
    assert cache_kt.shape[2:] == (SB_HEADS, SB_HEAD_DIM, KEY_BLOCK)
    pps = SAMPLE_PAGES_PER_STEP
    assert n_pages % pps == 0

    def page_spec(i):
        return pl.BlockSpec((None, None, SB_HEADS, SB_HEAD_DIM, KEY_BLOCK),
                            lambda b, s, pt: (layer, pt[b, n_pages - 1 - (s * pps + i)], 0, 0, 0))

    kv_specs = [page_spec(i) for i in range(pps)]
    head_spec = pl.BlockSpec((1, SB_HEADS, SB_HEAD_DIM), lambda b, s, pt: (b, 0, 0))
    grid_spec = pltpu.PrefetchScalarGridSpec(
        num_scalar_prefetch=1,
        grid=(n, n_pages // pps),
        in_specs=[head_spec,
                  pl.BlockSpec(bias_col.shape, lambda b, s, pt: (0, 0)),
                  pl.BlockSpec(t2.shape, lambda b, s, pt: (0, 0))] + kv_specs + kv_specs,
        out_specs=head_spec,
        scratch_shapes=[pltpu.VMEM((SB_HEADS, KEY_BLOCK), F32)],
    )
    out = pl.pallas_call(
        functools.partial(_sample_attn_kernel, pages_per_step=pps),
        grid_spec=grid_spec,
        out_shape=jax.ShapeDtypeStruct((n, SB_HEADS, SB_HEAD_DIM), F32),
        compiler_params=pltpu.CompilerParams(dimension_semantics=("arbitrary", "arbitrary"),
                                             vmem_limit_bytes=VMEM_LIMIT),
        name="sample_attn",
    )(page_table, q.reshape(n, SB_HEADS, SB_HEAD_DIM), bias_col, t2,
      *([cache_kt] * pps), *([cache_vt] * pps))
    return out.reshape(n, SB_WIDTH)


def _sample_out_kernel(yc_ref, mab_ref, gc_ref, x_ref, wbrc_ref, wout_ref, gpost_ref, gpre_ref,
                       wup_ref, cw_ref, cb_ref, pre_ref, wdown_ref, gfpost_ref, o_ref, up_ref):
    ffn = wdown_ref.shape[0]
    m = mab_ref[...] + gc_ref[...] * jnp.dot(yc_ref[...].astype(BF16), wbrc_ref[...],
                                             preferred_element_type=F32)
    mixed = jnp.dot(m.astype(BF16), wout_ref[...], preferred_element_type=F32)
    x = x_ref[...] + _rms(mixed, gpost_ref[...])
    h = _rms(x, gpre_ref[...]).astype(BF16)

    def conv(lo, hi):
        up = jnp.dot(h, wup_ref[:, lo:hi], preferred_element_type=F32)
        up_ref[:, lo:hi] = up
        c = cb_ref[:, lo:hi] + up * cw_ref[CONV_WIDTH - 1:CONV_WIDTH, lo:hi]
        for j in range(CONV_WIDTH - 1):
            c = c + pre_ref[j, :, lo:hi] * cw_ref[j:j + 1, lo:hi]
        return c

    f = jnp.zeros(x.shape, F32)
    for lo, hi in _ffn_chunks(ffn):
        act = _gelu_tanh(conv(lo, hi)) * conv(ffn + lo, ffn + hi)
        f = f + jnp.dot(act.astype(BF16), wdown_ref[lo:hi, :], preferred_element_type=F32)
    o_ref[...] = x + _rms(f, gfpost_ref[...])


def _sample_out(yc, mab, gc, x, wbr_c, wout, gpost, gpre, wup, cw, cb, pre, wdown, gfpost):
    n, d = x.shape
    return pl.pallas_call(
        _sample_out_kernel,
        out_shape=(jax.ShapeDtypeStruct((n, d), F32),
                   jax.ShapeDtypeStruct((n, wup.shape[1]), F32)),
        compiler_params=pltpu.CompilerParams(vmem_limit_bytes=VMEM_LIMIT),
        name="sample_out",
    )(yc, mab, gc, x, wbr_c, wout, gpost, gpre, wup, cw, cb, pre, wdown, gfpost)


def kernel(x_prompt, x_sample, state_pool, cache_k, cache_v, page_table, state_conv,
           norm_mix_pre, norm_mix_post, norm_ffn_pre, norm_ffn_post, w_in, pool_w,
           pool_scale, sg_w, sg_b, sb_bias, w_branch, w_out, w_up, conv_w, conv_b, w_down):
    depth = w_in.shape[0]
    batch, seq, d = x_prompt.shape
    dec = x_sample.shape[0]
    page = cache_k.shape[2]
    past_len = page_table.shape[1] * page
    assert x_sample.shape[1] == 1 and past_len % CHUNK == 0 and page == KEY_BLOCK
    assert seq % TOKEN_TILE == 0 and seq % FFN_TOKEN_TILE == 0 and seq % QUERY_TILE == 0

    tri = _strict_lower(QUERY_TILE)
    t2 = _suffix_sum_matrix(KEY_BLOCK)
    cache_kt = _keys_minor(cache_k)
    cache_vt = _keys_minor(cache_v)
    pool_pre = jnp.transpose(state_pool, (0, 2, 1, 3))
    conv_pre = jnp.transpose(state_conv, (0, 2, 1, 3))

    xp = x_prompt.reshape(batch * seq, d)
    xs = x_sample.reshape(dec, d)
    outs = [[] for _ in range(9)]
    for l in range(depth):
        row = lambda w: w[l][None, :]
        win = w_in[l].astype(BF16)
        poolw = pool_w[l].astype(BF16)
        wbr = w_branch[l].astype(BF16)
        wout = w_out[l].astype(BF16)
        wup = w_up[l].astype(BF16)
        wdown = w_down[l].astype(BF16)
        pscale = row(pool_scale)
        cb = row(conv_b)
        g_pre, g_post = row(norm_mix_pre), row(norm_mix_post)
        gf_pre, gf_post = row(norm_ffn_pre), row(norm_ffn_post)

        q, k, v, kb, vb, mab, gc, pool = _prompt_in(
            xp, g_pre, win, poolw, pscale, sg_w[l], sg_b[l].T, wbr[:2], batch=batch, seq=seq)
        sh = lambda t: t.reshape(batch, seq, t.shape[-1])
        x1 = _prompt_attn(sb_bias[l], sh(q), sh(kb), sh(vb), tri, sh(mab), sh(gc), sh(xp),
                          wbr[2], wout, g_post)
        xp, conv = _prompt_ffn(x1.reshape(batch * seq, d), gf_pre, wup, conv_w[l], cb, wdown,
                               gf_post, batch=batch, seq=seq)

        sgw0 = jnp.repeat(sg_w[l][:, 0, 0], LANES)[None, :]
        sgb0 = jnp.repeat(sg_b[l][:, 0], LANES)[None, :]
        a_s, sgv, q_s, k_s, v_s, mab_s, gc_s = _sample_in(
            xs, g_pre, win, pool_pre[l], poolw, pscale, sgw0, sgb0, wbr[:2], pos0=past_len)
        yc_s = _sample_attn(page_table, q_s, sb_bias[l][:, None], t2, cache_kt, cache_vt, layer=l)
        xs, up_s = _sample_out(yc_s, mab_s, gc_s, xs, wbr[2], wout, g_post,
                               gf_pre, wup, conv_w[l], cb, conv_pre[l], wdown, gf_post)

        outs[0].append(pool[:, 1:])
        outs[1].append(jnp.concatenate([state_pool[l][:, 1:], a_s[:, None]], axis=1))
        outs[2].append(k.reshape(batch, seq, SB_HEADS, SB_HEAD_DIM))
        outs[3].append(v.reshape(batch, seq, SB_HEADS, SB_HEAD_DIM))
        outs[4].append(k_s.reshape(dec, 1, SB_HEADS, SB_HEAD_DIM))
        outs[5].append(v_s.reshape(dec, 1, SB_HEADS, SB_HEAD_DIM))
        outs[6].append(sgv[:, None])
        outs[7].append(conv[:, CONV_PREFIX - (CONV_WIDTH - 1):])
        outs[8].append(jnp.concatenate([state_conv[l][:, 1:], up_s[:, None]], axis=1))

    return (xp.reshape(batch, seq, d), xs.reshape(dec, 1, d)) + tuple(jnp.stack(o) for o in outs)
```

```python
import functools

import jax
import jax.numpy as jnp
from jax import lax
from jax.experimental import pallas as pl
from jax.experimental.pallas import tpu as pltpu

F32 = jnp.float32
BF16 = jnp.bfloat16

LANES = 128
POOL_WINDOWS = (2, 4, 8, 16)
POOL_PREFIX = max(POOL_WINDOWS)
POOL_BUF = POOL_PREFIX - 1
CHUNK = 128
SB_HEADS = 8
SB_HEAD_DIM = 64
SB_WIDTH = SB_HEADS * SB_HEAD_DIM
SB_SCALE = SB_HEAD_DIM ** -0.5
KEY_BLOCK = 128
CONV_WIDTH = 3
CONV_PREFIX = 8
EPS = 1e-6
LOG2E = 1.4426950408889634
VMEM_LIMIT = 56 * 1024 * 1024

TOKEN_TILE = 256
FFN_TOKEN_TILE = 256
QUERY_TILE = 256
FFN_CHUNK = 256
FFN_LOOKAHEAD = 3
SAMPLE_PAGES_PER_STEP = 8


def _rms(x, g):
    return x * lax.rsqrt(jnp.mean(x * x, axis=-1, keepdims=True) + EPS) * g


def _softplus(z):
    return jnp.maximum(z, 0.0) + jnp.log(1.0 + jnp.exp2(jnp.abs(z) * -LOG2E))


def _gelu_tanh(x):
    return 0.5 * x * (1.0 + jnp.tanh(0.7978845608028654 * (x + 0.044715 * (x * x * x))))


def _const_spec(shape):
    nd = len(shape)
    return pl.BlockSpec(shape, lambda *_: (0,) * nd, pipeline_mode=pl.Buffered(1))


def _suffix_sum_matrix(n):
    return jnp.concatenate([_strict_lower(n), jnp.ones((n, n), BF16)], axis=1)


def _strict_lower(n):
    r = lax.broadcasted_iota(jnp.int32, (n, n), 0)
    c = lax.broadcasted_iota(jnp.int32, (n, n), 1)
    return (r > c).astype(BF16)


def _lower(n):
    r = lax.broadcasted_iota(jnp.int32, (n, n), 0)
    c = lax.broadcasted_iota(jnp.int32, (n, n), 1)
    return (r >= c).astype(BF16)


def _prompt_in_kernel(x_ref, g_ref, win_ref, poolw_ref, pscale_ref, sgw_ref, sgb_ref, wbr_ref,
                      q_ref, k_ref, v_ref, kb_ref, vb_ref, mab_ref, gc_ref, pool_ref,
                      ext_ref, *, tiles_per_seq):
    tm = x_ref.shape[0]
    i = pl.program_id(0)
    t0 = (i % tiles_per_seq) * tm

    @pl.when(i % tiles_per_seq == 0)
    def _():
        ext_ref[0:POOL_PREFIX, :] = jnp.zeros((POOL_PREFIX, ext_ref.shape[1]), F32)

    h = _rms(x_ref[...], g_ref[...]).astype(BF16)

    def proj(lo, hi):
        return jnp.dot(h, win_ref[:, lo:hi], preferred_element_type=F32)

    ext_ref[POOL_PREFIX:, :] = proj(0, 512)
    pos1 = (t0 + 1 + lax.broadcasted_iota(jnp.int32, (tm, LANES), 0)).astype(F32)
    ya = []
    for gi, w in enumerate(POOL_WINDOWS):
        sl = slice(gi * LANES, (gi + 1) * LANES)
        cur = ext_ref[POOL_PREFIX:POOL_PREFIX + tm, sl]
        s = cur
        for j in range(1, w):
            s = s + ext_ref[POOL_PREFIX - j:POOL_PREFIX - j + tm, sl]
        d = (s / jnp.minimum(pos1, float(w)) - cur).astype(BF16)
        ya.append(jnp.dot(d, poolw_ref[gi], preferred_element_type=F32) * pscale_ref[:, sl])
    ya = jnp.concatenate(ya, axis=1).astype(BF16)
    last = ext_ref[tm:tm + POOL_PREFIX, :]
    pool_ref[0] = last
    ext_ref[0:POOL_PREFIX, :] = last

    u = proj(512, 1024)
    sv = proj(1024, 1536).astype(BF16)
    r = lax.broadcasted_iota(jnp.int32, (CHUNK, CHUNK), 0)
    c = lax.broadcasted_iota(jnp.int32, (CHUNK, CHUNK), 1)
    cols = []
    for gi in range(4):
        sl = slice(gi * LANES, (gi + 1) * LANES)
        wg = jnp.where(r >= c, sgw_ref[gi], 0.0).astype(BF16)
        bcol = sgb_ref[:, gi:gi + 1]
        rows = []
        for ci in range(tm // CHUNK):
            rs = slice(ci * CHUNK, (ci + 1) * CHUNK)
            mixed = jnp.dot(wg, sv[rs, sl], preferred_element_type=F32) + bcol
            rows.append(u[rs, sl] * mixed)
        cols.append(jnp.concatenate(rows, axis=0))
    yb = jnp.concatenate(cols, axis=1).astype(BF16)

    q_ref[...] = (proj(1536, 2048) * SB_SCALE).astype(BF16)
    k = proj(2048, 2560)
    k_ref[...] = k
    kb_ref[...] = k.astype(BF16)
    v = proj(2560, 3072)
    v_ref[...] = v
    vb_ref[...] = v.astype(BF16)

    ga = jax.nn.sigmoid(proj(3072, 4096))
    gb = jax.nn.sigmoid(proj(4096, 5120))
    gc_ref[...] = jax.nn.sigmoid(proj(5120, 6144)).astype(BF16)
    mab_ref[...] = (ga * jnp.dot(ya, wbr_ref[0], preferred_element_type=F32)
                    + gb * jnp.dot(yb, wbr_ref[1], preferred_element_type=F32))


def _prompt_in(x2, g, win, poolw, pscale, sgw, sgb_t, wbr_ab, *, batch, seq):
    n, d = x2.shape
    tm = TOKEN_TILE
    tps = seq // tm
    row = lambda w: pl.BlockSpec((tm, w), lambda i: (i, 0))
    out_shape = (
        jax.ShapeDtypeStruct((n, SB_WIDTH), BF16),
        jax.ShapeDtypeStruct((n, SB_WIDTH), F32),
        jax.ShapeDtypeStruct((n, SB_WIDTH), F32),
        jax.ShapeDtypeStruct((n, SB_WIDTH), BF16),
        jax.ShapeDtypeStruct((n, SB_WIDTH), BF16),
        jax.ShapeDtypeStruct((n, d), F32),
        jax.ShapeDtypeStruct((n, d), BF16),
        jax.ShapeDtypeStruct((batch, POOL_PREFIX, 512), F32),
    )
    out_specs = (row(SB_WIDTH), row(SB_WIDTH), row(SB_WIDTH), row(SB_WIDTH), row(SB_WIDTH),
                 row(d), row(d),
                 pl.BlockSpec((1, POOL_PREFIX, 512), lambda i: (i // tps, 0, 0)))
    return pl.pallas_call(
        functools.partial(_prompt_in_kernel, tiles_per_seq=tps),
        grid=(n // tm,),
        in_specs=[row(d), _const_spec(g.shape), _const_spec(win.shape), _const_spec(poolw.shape),
                  _const_spec(pscale.shape), _const_spec(sgw.shape), _const_spec(sgb_t.shape),
                  _const_spec(wbr_ab.shape)],
        out_specs=out_specs,
        out_shape=out_shape,
        scratch_shapes=[pltpu.VMEM((tm + POOL_PREFIX, 512), F32)],
        compiler_params=pltpu.CompilerParams(dimension_semantics=("arbitrary",),
                                             vmem_limit_bytes=VMEM_LIMIT),
        name="prompt_in",
    )(x2, g, win, poolw, pscale, sgw, sgb_t, wbr_ab)


def _prompt_attn_kernel(bias_ref, q_ref, kb_ref, vb_ref, tri_ref, mab_ref, gc_ref, x_ref,
                        wbrc_ref, wout_ref, g_ref, o_ref,
                        qm_ref, z_ref, spb_ref, ab_ref, acc_ref, carry_ref):
    tq = q_ref.shape[1]
    tk = tri_ref.shape[0]
    q0 = pl.program_id(1) * tq

    lane = lax.broadcasted_iota(jnp.int32, (tq, LANES), 1)
    low = lane < SB_HEAD_DIM
    for p in range(SB_HEADS // 2):
        qt = q_ref[0, :, p * LANES:(p + 1) * LANES]
        qm_ref[2 * p] = jnp.where(low, qt, 0.0).astype(BF16)
        qm_ref[2 * p + 1] = jnp.where(low, 0.0, qt).astype(BF16)
    acc_ref[...] = jnp.zeros(acc_ref.shape, F32)
    carry_ref[...] = jnp.zeros(carry_ref.shape, F32)

    def tile(ref, k0, h):
        return ref[0, pl.ds(k0, tk), (h // 2) * LANES:(h // 2 + 1) * LANES]

    def logits(k0, slot, h, mask):
        z = lax.dot_general(qm_ref[h], tile(kb_ref, k0, h), (((1,), (1,)), ((), ())),
                            preferred_element_type=F32) + bias_ref[h]
        sp = _softplus(z)
        if mask is not None:
            sp = jnp.where(mask, sp, 0.0)
            z = jnp.where(mask, z, -jnp.inf)
        z_ref[slot, h] = z
        spb_ref[slot, h] = sp.astype(BF16)

    def suffix(slot, h):
        return jnp.dot(spb_ref[slot, h], tri_ref[...], preferred_element_type=F32)

    def weights(slot, h, c):
        cr = carry_ref[h]
        a = jnp.exp(z_ref[slot, h] - c - jnp.concatenate([cr] * (tk // LANES), axis=1))
        ab_ref[h] = a.astype(BF16)
        carry_ref[h] = cr + jnp.broadcast_to(c[:, 0:1], cr.shape)

    def accumulate(k0, h):
        acc_ref[h] += jnp.dot(ab_ref[h], tile(vb_ref, k0, h), preferred_element_type=F32)

    def stage(apply_block, logits_block):
        c_prev = None
        for h in range(SB_HEADS + 1):
            if logits_block is not None and h < SB_HEADS:
                logits(logits_block[0], logits_block[1], h, None)
            if apply_block is not None:
                c = suffix(apply_block[1], h) if h < SB_HEADS else None
                if h >= 1:
                    weights(apply_block[1], h - 1, c_prev)
                    accumulate(apply_block[0], h - 1)
                c_prev = c

    def k_start(j):
        return pl.multiple_of(q0 - j * tk, tk)

    assert tq == tk
    n_apply = q0 // tk
    causal = (lax.broadcasted_iota(jnp.int32, (tq, tk), 1)
              < lax.broadcasted_iota(jnp.int32, (tq, tk), 0))
    for h in range(SB_HEADS):
        logits(k_start(0), 0, h, causal)

    def body(i, _):
        j = 2 * i
        stage((k_start(j), 0), (k_start(j + 1), 1))
        stage((k_start(j + 1), 1), (k_start(j + 2), 0))
        return 0

    lax.fori_loop(0, n_apply // 2, body, 0)

    @pl.when(n_apply % 2 == 1)
    def _():
        stage((k_start(n_apply - 1), 0), (k_start(n_apply), 1))
        stage((k_start(n_apply), 1), None)

    @pl.when(n_apply % 2 == 0)
    def _():
        stage((k_start(n_apply), 0), None)

    tiles = []
    for p in range(SB_HEADS // 2):
        tiles.append(jnp.where(low, acc_ref[2 * p], acc_ref[2 * p + 1]))
    yc = jnp.concatenate(tiles, axis=1).astype(BF16)
    m = mab_ref[0] + gc_ref[0].astype(F32) * jnp.dot(yc, wbrc_ref[...], preferred_element_type=F32)
    mixed = jnp.dot(m.astype(BF16), wout_ref[...], preferred_element_type=F32)
    o_ref[0] = x_ref[0] + _rms(mixed, g_ref[...])


def _prompt_attn(bias, q, kb, vb, tri, mab, gc, x, wbr_c, wout, g_post):
    b, s, d = x.shape
    tq = QUERY_TILE
    tk = tri.shape[0]
    row = lambda w: pl.BlockSpec((1, tq, w), lambda bi, qi: (bi, qi, 0))
    seq_spec = pl.BlockSpec((1, s, SB_WIDTH), lambda bi, qi: (bi, 0, 0), pipeline_mode=pl.Buffered(1))
    return pl.pallas_call(
        _prompt_attn_kernel,
        grid=(b, s // tq),
        in_specs=[pl.BlockSpec(memory_space=pltpu.SMEM),
                  row(SB_WIDTH), seq_spec, seq_spec, _const_spec(tri.shape),
                  row(d), row(d), row(d),
                  _const_spec(wbr_c.shape), _const_spec(wout.shape), _const_spec(g_post.shape)],
        out_specs=row(d),
        out_shape=jax.ShapeDtypeStruct((b, s, d), F32),
        scratch_shapes=[pltpu.VMEM((SB_HEADS, tq, LANES), BF16),
                        pltpu.VMEM((2, SB_HEADS, tq, tk), F32),
                        pltpu.VMEM((2, SB_HEADS, tq, tk), BF16),
                        pltpu.VMEM((SB_HEADS, tq, tk), BF16),
                        pltpu.VMEM((SB_HEADS, tq, LANES), F32),
                        pltpu.VMEM((SB_HEADS, tq, LANES), F32)],
        compiler_params=pltpu.CompilerParams(dimension_semantics=("arbitrary", "arbitrary"),
                                             vmem_limit_bytes=VMEM_LIMIT),
        name="prompt_attn",
    )(bias, q, kb, vb, tri, mab, gc, x, wbr_c, wout, g_post)


def _ffn_chunks(ffn_dim):
    assert ffn_dim % LANES == 0
    edges = list(range(0, ffn_dim, FFN_CHUNK)) + [ffn_dim]
    return list(zip(edges[:-1], edges[1:]))


def _prompt_ffn_kernel(x_ref, gpre_ref, wup_ref, cw_ref, cb_ref, wdown_ref, gpost_ref,
                       o_ref, conv_ref, ext_ref, *, tiles_per_seq):
    tm = x_ref.shape[0]
    ffn = wdown_ref.shape[0]
    i = pl.program_id(0)

    @pl.when(i % tiles_per_seq == 0)
    def _():
        ext_ref[0:CONV_PREFIX, :] = jnp.zeros((CONV_PREFIX, ext_ref.shape[1]), F32)

    x = x_ref[...]
    h = _rms(x, gpre_ref[...]).astype(BF16)

    def up(lo, hi):
        ext_ref[CONV_PREFIX:, lo:hi] = jnp.dot(h, wup_ref[:, lo:hi], preferred_element_type=F32)

    def conv(lo, hi):
        c = cb_ref[:, lo:hi]
        for j in range(CONV_WIDTH):
            s = CONV_PREFIX - (CONV_WIDTH - 1) + j
            c = c + ext_ref[s:s + tm, lo:hi] * cw_ref[j:j + 1, lo:hi]
        return c

    chunks = _ffn_chunks(ffn)
    f = jnp.zeros((tm, o_ref.shape[1]), F32)
    pending = None
    for i in range(len(chunks) + FFN_LOOKAHEAD + 1):
        if i < len(chunks):
            lo, hi = chunks[i]
            up(lo, hi)
            up(ffn + lo, ffn + hi)
        if pending is not None:
            act, lo, hi = pending
            f = f + jnp.dot(act, wdown_ref[lo:hi, :], preferred_element_type=F32)
            pending = None
        if FFN_LOOKAHEAD <= i < len(chunks) + FFN_LOOKAHEAD:
            lo, hi = chunks[i - FFN_LOOKAHEAD]
            act = _gelu_tanh(conv(lo, hi)) * conv(ffn + lo, ffn + hi)
            pending = (act.astype(BF16), lo, hi)
    o_ref[...] = x + _rms(f, gpost_ref[...])

    last = ext_ref[tm:tm + CONV_PREFIX, :]
    conv_ref[0] = last
    ext_ref[0:CONV_PREFIX, :] = last


def _prompt_ffn(x2, gpre, wup, cw, cb, wdown, gpost, *, batch, seq):
    n, d = x2.shape
    tm = FFN_TOKEN_TILE
    tps = seq // tm
    f2 = wup.shape[1]
    return pl.pallas_call(
        functools.partial(_prompt_ffn_kernel, tiles_per_seq=tps),
        grid=(n // tm,),
        in_specs=[pl.BlockSpec((tm, d), lambda i: (i, 0)),
                  _const_spec(gpre.shape), _const_spec(wup.shape), _const_spec(cw.shape),
                  _const_spec(cb.shape), _const_spec(wdown.shape), _const_spec(gpost.shape)],
        out_specs=(pl.BlockSpec((tm, d), lambda i: (i, 0)),
                   pl.BlockSpec((1, CONV_PREFIX, f2), lambda i: (i // tps, 0, 0))),
        out_shape=(jax.ShapeDtypeStruct((n, d), F32),
                   jax.ShapeDtypeStruct((batch, CONV_PREFIX, f2), F32)),
        scratch_shapes=[pltpu.VMEM((tm + CONV_PREFIX, f2), F32)],
        compiler_params=pltpu.CompilerParams(dimension_semantics=("arbitrary",),
                                             vmem_limit_bytes=VMEM_LIMIT),
        name="prompt_ffn",
    )(x2, gpre, wup, cw, cb, wdown, gpost)


def _sample_in_kernel(x_ref, g_ref, win_ref, pre_ref, poolw_ref, pscale_ref, sgw0_ref, sgb0_ref,
                      wbr_ref, a_ref, sgv_ref, q_ref, k_ref, v_ref, mab_ref, gc_ref, *, pos0):
    h = _rms(x_ref[...], g_ref[...]).astype(BF16)

    def proj(lo, hi):
        return jnp.dot(h, win_ref[:, lo:hi], preferred_element_type=F32)

    a = proj(0, 512)
    a_ref[...] = a
    ya = []
    for gi, w in enumerate(POOL_WINDOWS):
        sl = slice(gi * LANES, (gi + 1) * LANES)
        s = a[:, sl]
        for j in range(1, w):
            s = s + pre_ref[POOL_BUF - j, :, sl]
        d = (s / float(min(pos0 + 1, w)) - a[:, sl]).astype(BF16)
        ya.append(jnp.dot(d, poolw_ref[gi], preferred_element_type=F32) * pscale_ref[:, sl])
    ya = jnp.concatenate(ya, axis=1).astype(BF16)

    u = proj(512, 1024)
    sv = proj(1024, 1536)
    sgv_ref[...] = sv
    yb = (u * (sgw0_ref[...] * sv + sgb0_ref[...])).astype(BF16)

    q_ref[...] = proj(1536, 2048) * SB_SCALE
    k_ref[...] = proj(2048, 2560)
    v_ref[...] = proj(2560, 3072)
    ga = jax.nn.sigmoid(proj(3072, 4096))
    gb = jax.nn.sigmoid(proj(4096, 5120))
    gc_ref[...] = jax.nn.sigmoid(proj(5120, 6144))
    mab_ref[...] = (ga * jnp.dot(ya, wbr_ref[0], preferred_element_type=F32)
                    + gb * jnp.dot(yb, wbr_ref[1], preferred_element_type=F32))


def _sample_in(x, g, win, pre, poolw, pscale, sgw0, sgb0, wbr_ab, *, pos0):
    n, d = x.shape
    sd = lambda w: jax.ShapeDtypeStruct((n, w), F32)
    return pl.pallas_call(
        functools.partial(_sample_in_kernel, pos0=pos0),
        out_shape=(sd(512), sd(512), sd(SB_WIDTH), sd(SB_WIDTH), sd(SB_WIDTH), sd(d), sd(d)),
        compiler_params=pltpu.CompilerParams(vmem_limit_bytes=VMEM_LIMIT),
        name="sample_in",
    )(x, g, win, pre, poolw, pscale, sgw0, sgb0, wbr_ab)


def _sample_attn_kernel(pt_ref, q_ref, bias_ref, t2_ref, *refs, pages_per_step):
    del pt_ref
    k_refs = refs[:pages_per_step]
    v_refs = refs[pages_per_step:2 * pages_per_step]
    o_ref, carry_ref = refs[2 * pages_per_step:]

    @pl.when(pl.program_id(1) == 0)
    def _():
        o_ref[...] = jnp.zeros(o_ref.shape, F32)
        carry_ref[...] = jnp.zeros(carry_ref.shape, F32)

    q8 = q_ref[0].astype(BF16)
    row_k = lax.broadcasted_iota(jnp.int32, (SB_HEADS, KEY_BLOCK), 0)
    row_d = lax.broadcasted_iota(jnp.int32, (SB_HEADS, SB_HEAD_DIM), 0)

    logits = []
    for i in range(pages_per_step):
        z = jnp.zeros((SB_HEADS, KEY_BLOCK), F32)
        for h in range(SB_HEADS):
            res = jnp.dot(q8, k_refs[i][h].astype(BF16), preferred_element_type=F32)
            z = jnp.where(row_k == h, res, z)
        logits.append(z + bias_ref[...])
    softplus = [_softplus(z) for z in logits]
    sums = [jnp.dot(sp.astype(BF16), t2_ref[...], preferred_element_type=F32) for sp in softplus]
    carry = carry_ref[...]
    weights = []
    for z, sp, r in zip(logits, softplus, sums):
        weights.append(jnp.exp(z - sp - r[:, :KEY_BLOCK] - carry).astype(BF16))
        carry = carry + r[:, KEY_BLOCK:]
    carry_ref[...] = carry
    out = o_ref[0]
    for i in range(pages_per_step):
        for h in range(SB_HEADS):
            res = lax.dot_general(weights[i], v_refs[i][h].astype(BF16), (((1,), (1,)), ((), ())),
                                  preferred_element_type=F32)
            out = out + jnp.where(row_d == h, res, 0.0)
    o_ref[0] = out


def _keys_minor(cache):
    return jnp.transpose(cache, (0, 1, 3, 4, 2))


def _sample_attn(page_table, q, bias_col, t2, cache_kt, cache_vt, *, layer):
    n, n_pages = page_table.shape
    assert cache_kt.shape[2:] == (SB_HEADS, SB_HEAD_DIM, KEY_BLOCK)
    pps = SAMPLE_PAGES_PER_STEP
    assert n_pages % pps == 0

    def page_spec(i):
        return pl.BlockSpec((None, None, SB_HEADS, SB_HEAD_DIM, KEY_BLOCK),
                            lambda b, s, pt: (layer, pt[b, n_pages - 1 - (s * pps + i)], 0, 0, 0))

    kv_specs = [page_spec(i) for i in range(pps)]
    head_spec = pl.BlockSpec((1, SB_HEADS, SB_HEAD_DIM), lambda b, s, pt: (b, 0, 0))
    grid_spec = pltpu.PrefetchScalarGridSpec(
        num_scalar_prefetch=1,
        grid=(n, n_pages // pps),
        in_specs=[head_spec,
                  pl.BlockSpec(bias_col.shape, lambda b, s, pt: (0, 0)),
                  pl.BlockSpec(t2.shape, lambda b, s, pt: (0, 0))] + kv_specs + kv_specs,
        out_specs=head_spec,
        scratch_shapes=[pltpu.VMEM((SB_HEADS, KEY_BLOCK), F32)],
    )
    out = pl.pallas_call(
        functools.partial(_sample_attn_kernel, pages_per_step=pps),
        grid_spec=grid_spec,
        out_shape=jax.ShapeDtypeStruct((n, SB_HEADS, SB_HEAD_DIM), F32),
        compiler_params=pltpu.CompilerParams(dimension_semantics=("arbitrary", "arbitrary"),
                                             vmem_limit_bytes=VMEM_LIMIT),
        name="sample_attn",
    )(page_table, q.reshape(n, SB_HEADS, SB_HEAD_DIM), bias_col, t2,
      *([cache_kt] * pps), *([cache_vt] * pps))
    return out.reshape(n, SB_WIDTH)


def _sample_out_kernel(yc_ref, mab_ref, gc_ref, x_ref, wbrc_ref, wout_ref, gpost_ref, gpre_ref,
                       wup_ref, cw_ref, cb_ref, pre_ref, wdown_ref, gfpost_ref, o_ref, up_ref):
    ffn = wdown_ref.shape[0]
    m = mab_ref[...] + gc_ref[...] * jnp.dot(yc_ref[...].astype(BF16), wbrc_ref[...],
                                             preferred_element_type=F32)
    mixed = jnp.dot(m.astype(BF16), wout_ref[...], preferred_element_type=F32)
    x = x_ref[...] + _rms(mixed, gpost_ref[...])
    h = _rms(x, gpre_ref[...]).astype(BF16)

    def conv(lo, hi):
        up = jnp.dot(h, wup_ref[:, lo:hi], preferred_element_type=F32)
        up_ref[:, lo:hi] = up
        c = cb_ref[:, lo:hi] + up * cw_ref[CONV_WIDTH - 1:CONV_WIDTH, lo:hi]
        for j in range(CONV_WIDTH - 1):
            c = c + pre_ref[j, :, lo:hi] * cw_ref[j:j + 1, lo:hi]
        return c

    f = jnp.zeros(x.shape, F32)
    for lo, hi in _ffn_chunks(ffn):
        act = _gelu_tanh(conv(lo, hi)) * conv(ffn + lo, ffn + hi)
        f = f + jnp.dot(act.astype(BF16), wdown_ref[lo:hi, :], preferred_element_type=F32)
    o_ref[...] = x + _rms(f, gfpost_ref[...])


def _sample_out(yc, mab, gc, x, wbr_c, wout, gpost, gpre, wup, cw, cb, pre, wdown, gfpost):
    n, d = x.shape
    return pl.pallas_call(
        _sample_out_kernel,
        out_shape=(jax.ShapeDtypeStruct((n, d), F32),
                   jax.ShapeDtypeStruct((n, wup.shape[1]), F32)),
        compiler_params=pltpu.CompilerParams(vmem_limit_bytes=VMEM_LIMIT),
        name="sample_out",
    )(yc, mab, gc, x, wbr_c, wout, gpost, gpre, wup, cw, cb, pre, wdown, gfpost)


def kernel(x_prompt, x_sample, state_pool, cache_k, cache_v, page_table, state_conv,
           norm_mix_pre, norm_mix_post, norm_ffn_pre, norm_ffn_post, w_in, pool_w,
           pool_scale, sg_w, sg_b, sb_bias, w_branch, w_out, w_up, conv_w, conv_b, w_down):
    depth = w_in.shape[0]
    batch, seq, d = x_prompt.shape
    dec = x_sample.shape[0]
    page = cache_k.shape[2]
    past_len = page_table.shape[1] * page
    assert x_sample.shape[1] == 1 and past_len % CHUNK == 0 and page == KEY_BLOCK
    assert seq % TOKEN_TILE == 0 and seq % FFN_TOKEN_TILE == 0 and seq % QUERY_TILE == 0

    tri = _lower(QUERY_TILE)
    t2 = _suffix_sum_matrix(KEY_BLOCK)
    cache_kt = _keys_minor(cache_k)
    cache_vt = _keys_minor(cache_v)
    pool_pre = jnp.transpose(state_pool, (0, 2, 1, 3))
    conv_pre = jnp.transpose(state_conv, (0, 2, 1, 3))

    xp = x_prompt.reshape(batch * seq, d)
    xs = x_sample.reshape(dec, d)
    outs = [[] for _ in range(9)]
    for l in range(depth):
        row = lambda w: w[l][None, :]
        win = w_in[l].astype(BF16)
        poolw = pool_w[l].astype(BF16)
        wbr = w_branch[l].astype(BF16)
        wout = w_out[l].astype(BF16)
        wup = w_up[l].astype(BF16)
        wdown = w_down[l].astype(BF16)
        pscale = row(pool_scale)
        cb = row(conv_b)
        g_pre, g_post = row(norm_mix_pre), row(norm_mix_post)
        gf_pre, gf_post = row(norm_ffn_pre), row(norm_ffn_post)

        q, k, v, kb, vb, mab, gc, pool = _prompt_in(
            xp, g_pre, win, poolw, pscale, sg_w[l], sg_b[l].T, wbr[:2], batch=batch, seq=seq)
        sh = lambda t: t.reshape(batch, seq, t.shape[-1])
        x1 = _prompt_attn(sb_bias[l], sh(q), sh(kb), sh(vb), tri, sh(mab), sh(gc), sh(xp),
                          wbr[2], wout, g_post)
        xp, conv = _prompt_ffn(x1.reshape(batch * seq, d), gf_pre, wup, conv_w[l], cb, wdown,
                               gf_post, batch=batch, seq=seq)

        sgw0 = jnp.repeat(sg_w[l][:, 0, 0], LANES)[None, :]
        sgb0 = jnp.repeat(sg_b[l][:, 0], LANES)[None, :]
        a_s, sgv, q_s, k_s, v_s, mab_s, gc_s = _sample_in(
            xs, g_pre, win, pool_pre[l], poolw, pscale, sgw0, sgb0, wbr[:2], pos0=past_len)
        yc_s = _sample_attn(page_table, q_s, sb_bias[l][:, None], t2, cache_kt, cache_vt, layer=l)
        xs, up_s = _sample_out(yc_s, mab_s, gc_s, xs, wbr[2], wout, g_post,
                               gf_pre, wup, conv_w[l], cb, conv_pre[l], wdown, gf_post)

        outs[0].append(pool[:, 1:])
        outs[1].append(jnp.concatenate([state_pool[l][:, 1:], a_s[:, None]], axis=1))
        outs[2].append(k.reshape(batch, seq, SB_HEADS, SB_HEAD_DIM))
        outs[3].append(v.reshape(batch, seq, SB_HEADS, SB_HEAD_DIM))
        outs[4].append(k_s.reshape(dec, 1, SB_HEADS, SB_HEAD_DIM))
        outs[5].append(v_s.reshape(dec, 1, SB_HEADS, SB_HEAD_DIM))
        outs[6].append(sgv[:, None])
        outs[7].append(conv[:, CONV_PREFIX - (CONV_WIDTH - 1):])
        outs[8].append(jnp.concatenate([state_conv[l][:, 1:], up_s[:, None]], axis=1))

    return (xp.reshape(batch, seq, d), xs.reshape(dec, 1, d)) + tuple(jnp.stack(o) for o in outs)
```

```python
import functools

import jax
import jax.numpy as jnp
from jax import lax
from jax.experimental import pallas as pl
from jax.experimental.pallas import tpu as pltpu

F32 = jnp.float32
BF16 = jnp.bfloat16

LANES = 128
POOL_WINDOWS = (2, 4, 8, 16)
POOL_PREFIX = max(POOL_WINDOWS)
POOL_BUF = POOL_PREFIX - 1
CHUNK = 128
SB_HEADS = 8
SB_HEAD_DIM = 64
SB_WIDTH = SB_HEADS * SB_HEAD_DIM
SB_SCALE = SB_HEAD_DIM ** -0.5
KEY_BLOCK = 128
CONV_WIDTH = 3
CONV_PREFIX = 8
EPS = 1e-6
LOG2E = 1.4426950408889634
VMEM_LIMIT = 56 * 1024 * 1024

TOKEN_TILE = 256
FFN_TOKEN_TILE = 256
QUERY_TILE = 256
FFN_CHUNK = 256
FFN_LOOKAHEAD = 3
SAMPLE_PAGES_PER_STEP = 16


def _rms(x, g):
    return x * lax.rsqrt(jnp.mean(x * x, axis=-1, keepdims=True) + EPS) * g


def _softplus(z):
    return jnp.maximum(z, 0.0) + jnp.log(1.0 + jnp.exp2(jnp.abs(z) * -LOG2E))


def _gelu_tanh(x):
    return 0.5 * x * (1.0 + jnp.tanh(0.7978845608028654 * (x + 0.044715 * (x * x * x))))


def _const_spec(shape):
    nd = len(shape)
    return pl.BlockSpec(shape, lambda *_: (0,) * nd, pipeline_mode=pl.Buffered(1))


def _suffix_sum_matrix(n):
    return jnp.concatenate([_strict_lower(n), jnp.ones((n, n), BF16)], axis=1)


def _strict_lower(n):
    r = lax.broadcasted_iota(jnp.int32, (n, n), 0)
    c = lax.broadcasted_iota(jnp.int32, (n, n), 1)
    return (r > c).astype(BF16)


def _lower(n):
    r = lax.broadcasted_iota(jnp.int32, (n, n), 0)
    c = lax.broadcasted_iota(jnp.int32, (n, n), 1)
    return (r >= c).astype(BF16)


def _prompt_in_kernel(x_ref, g_ref, win_ref, poolw_ref, pscale_ref, sgw_ref, sgb_ref, wbr_ref,
                      q_ref, k_ref, v_ref, kb_ref, vb_ref, mab_ref, gc_ref, pool_ref,
                      ext_ref, *, tiles_per_seq):
    tm = x_ref.shape[0]
    i = pl.program_id(0)
    t0 = (i % tiles_per_seq) * tm

    @pl.when(i % tiles_per_seq == 0)
    def _():
        ext_ref[0:POOL_PREFIX, :] = jnp.zeros((POOL_PREFIX, ext_ref.shape[1]), F32)

    h = _rms(x_ref[...], g_ref[...]).astype(BF16)

    def proj(lo, hi):
        return jnp.dot(h, win_ref[:, lo:hi], preferred_element_type=F32)

    ext_ref[POOL_PREFIX:, :] = proj(0, 512)
    pos1 = (t0 + 1 + lax.broadcasted_iota(jnp.int32, (tm, LANES), 0)).astype(F32)
    ya = []
    for gi, w in enumerate(POOL_WINDOWS):
        sl = slice(gi * LANES, (gi + 1) * LANES)
        cur = ext_ref[POOL_PREFIX:POOL_PREFIX + tm, sl]
        s = cur
        for j in range(1, w):
            s = s + ext_ref[POOL_PREFIX - j:POOL_PREFIX - j + tm, sl]
        d = (s / jnp.minimum(pos1, float(w)) - cur).astype(BF16)
        ya.append(jnp.dot(d, poolw_ref[gi], preferred_element_type=F32) * pscale_ref[:, sl])
    ya = jnp.concatenate(ya, axis=1).astype(BF16)
    last = ext_ref[tm:tm + POOL_PREFIX, :]
    pool_ref[0] = last
    ext_ref[0:POOL_PREFIX, :] = last

    u = proj(512, 1024)
    sv = proj(1024, 1536).astype(BF16)
    r = lax.broadcasted_iota(jnp.int32, (CHUNK, CHUNK), 0)
    c = lax.broadcasted_iota(jnp.int32, (CHUNK, CHUNK), 1)
    cols = []
    for gi in range(4):
        sl = slice(gi * LANES, (gi + 1) * LANES)
        wg = jnp.where(r >= c, sgw_ref[gi], 0.0).astype(BF16)
        bcol = sgb_ref[:, gi:gi + 1]
        rows = []
        for ci in range(tm // CHUNK):
            rs = slice(ci * CHUNK, (ci + 1) * CHUNK)
            mixed = jnp.dot(wg, sv[rs, sl], preferred_element_type=F32) + bcol
            rows.append(u[rs, sl] * mixed)
        cols.append(jnp.concatenate(rows, axis=0))
    yb = jnp.concatenate(cols, axis=1).astype(BF16)

    q_ref[...] = (proj(1536, 2048) * SB_SCALE).astype(BF16)
    k = proj(2048, 2560)
    k_ref[0] = k.T
    kb_ref[...] = k.astype(BF16)
    v = proj(2560, 3072)
    v_ref[0] = v.T
    vb_ref[...] = v.astype(BF16)

    ga = jax.nn.sigmoid(proj(3072, 4096))
    gb = jax.nn.sigmoid(proj(4096, 5120))
    gc_ref[...] = jax.nn.sigmoid(proj(5120, 6144)).astype(BF16)
    mab_ref[...] = (ga * jnp.dot(ya, wbr_ref[0], preferred_element_type=F32)
                    + gb * jnp.dot(yb, wbr_ref[1], preferred_element_type=F32))


def _prompt_in(x2, g, win, poolw, pscale, sgw, sgb_t, wbr_ab, *, batch, seq):
    n, d = x2.shape
    tm = TOKEN_TILE
    tps = seq // tm
    row = lambda w: pl.BlockSpec((tm, w), lambda i: (i, 0))
    out_shape = (
        jax.ShapeDtypeStruct((n, SB_WIDTH), BF16),
        jax.ShapeDtypeStruct((batch, SB_WIDTH, seq), F32),
        jax.ShapeDtypeStruct((batch, SB_WIDTH, seq), F32),
        jax.ShapeDtypeStruct((n, SB_WIDTH), BF16),
        jax.ShapeDtypeStruct((n, SB_WIDTH), BF16),
        jax.ShapeDtypeStruct((n, d), F32),
        jax.ShapeDtypeStruct((n, d), BF16),
        jax.ShapeDtypeStruct((batch, POOL_PREFIX, 512), F32),
    )
    col = pl.BlockSpec((1, SB_WIDTH, tm), lambda i: (i // tps, 0, i % tps))
    out_specs = (row(SB_WIDTH), col, col, row(SB_WIDTH), row(SB_WIDTH),
                 row(d), row(d),
                 pl.BlockSpec((1, POOL_PREFIX, 512), lambda i: (i // tps, 0, 0)))
    return pl.pallas_call(
        functools.partial(_prompt_in_kernel, tiles_per_seq=tps),
        grid=(n // tm,),
        in_specs=[row(d), _const_spec(g.shape), _const_spec(win.shape), _const_spec(poolw.shape),
                  _const_spec(pscale.shape), _const_spec(sgw.shape), _const_spec(sgb_t.shape),
                  _const_spec(wbr_ab.shape)],
        out_specs=out_specs,
        out_shape=out_shape,
        scratch_shapes=[pltpu.VMEM((tm + POOL_PREFIX, 512), F32)],
        compiler_params=pltpu.CompilerParams(dimension_semantics=("arbitrary",),
                                             vmem_limit_bytes=VMEM_LIMIT),
        name="prompt_in",
    )(x2, g, win, poolw, pscale, sgw, sgb_t, wbr_ab)


def _prompt_attn_kernel(bias_ref, q_ref, kb_ref, vb_ref, tri_ref, mab_ref, gc_ref, x_ref,
                        wbrc_ref, wout_ref, g_ref, o_ref,
                        qm_ref, z_ref, spb_ref, ab_ref, acc_ref, carry_ref):
    tq = q_ref.shape[1]
    tk = tri_ref.shape[0]
    q0 = pl.program_id(1) * tq

    lane = lax.broadcasted_iota(jnp.int32, (tq, LANES), 1)
    low = lane < SB_HEAD_DIM
    for p in range(SB_HEADS // 2):
        qt = q_ref[0, :, p * LANES:(p + 1) * LANES]
        qm_ref[2 * p] = jnp.where(low, qt, 0.0).astype(BF16)
        qm_ref[2 * p + 1] = jnp.where(low, 0.0, qt).astype(BF16)
    acc_ref[...] = jnp.zeros(acc_ref.shape, F32)
    carry_ref[...] = jnp.zeros(carry_ref.shape, F32)

    def tile(ref, k0, h):
        return ref[0, pl.ds(k0, tk), (h // 2) * LANES:(h // 2 + 1) * LANES]

    def logits(k0, slot, h, mask):
        z = lax.dot_general(qm_ref[h], tile(kb_ref, k0, h), (((1,), (1,)), ((), ())),
                            preferred_element_type=F32) + bias_ref[h]
        sp = _softplus(z)
        if mask is not None:
            sp = jnp.where(mask, sp, 0.0)
            z = jnp.where(mask, z, -jnp.inf)
        z_ref[slot, h] = z
        spb_ref[slot, h] = sp.astype(BF16)

    def suffix(slot, h):
        return jnp.dot(spb_ref[slot, h], tri_ref[...], preferred_element_type=F32)

    def weights(slot, h, c):
        cr = carry_ref[h]
        a = jnp.exp(z_ref[slot, h] - c - jnp.concatenate([cr] * (tk // LANES), axis=1))
        ab_ref[h] = a.astype(BF16)
        carry_ref[h] = cr + jnp.broadcast_to(c[:, 0:1], cr.shape)

    def accumulate(k0, h):
        acc_ref[h] += jnp.dot(ab_ref[h], tile(vb_ref, k0, h), preferred_element_type=F32)

    def stage(apply_block, logits_block):
        c_prev = None
        for h in range(SB_HEADS + 1):
            if logits_block is not None and h < SB_HEADS:
                logits(logits_block[0], logits_block[1], h, None)
            if apply_block is not None:
                c = suffix(apply_block[1], h) if h < SB_HEADS else None
                if h >= 1:
                    weights(apply_block[1], h - 1, c_prev)
                    accumulate(apply_block[0], h - 1)
                c_prev = c

    def k_start(j):
        return pl.multiple_of(q0 - j * tk, tk)

    assert tq == tk
    n_apply = q0 // tk
    causal = (lax.broadcasted_iota(jnp.int32, (tq, tk), 1)
              < lax.broadcasted_iota(jnp.int32, (tq, tk), 0))
    for h in range(SB_HEADS):
        logits(k_start(0), 0, h, causal)

    def body(i, _):
        j = 2 * i
        stage((k_start(j), 0), (k_start(j + 1), 1))
        stage((k_start(j + 1), 1), (k_start(j + 2), 0))
        return 0

    lax.fori_loop(0, n_apply // 2, body, 0)

    @pl.when(n_apply % 2 == 1)
    def _():
        stage((k_start(n_apply - 1), 0), (k_start(n_apply), 1))
        stage((k_start(n_apply), 1), None)

    @pl.when(n_apply % 2 == 0)
    def _():
        stage((k_start(n_apply), 0), None)

    tiles = []
    for p in range(SB_HEADS // 2):
        tiles.append(jnp.where(low, acc_ref[2 * p], acc_ref[2 * p + 1]))
    yc = jnp.concatenate(tiles, axis=1).astype(BF16)
    m = mab_ref[0] + gc_ref[0].astype(F32) * jnp.dot(yc, wbrc_ref[...], preferred_element_type=F32)
    mixed = jnp.dot(m.astype(BF16), wout_ref[...], preferred_element_type=F32)
    o_ref[0] = x_ref[0] + _rms(mixed, g_ref[...])


def _prompt_attn(bias, q, kb, vb, tri, mab, gc, x, wbr_c, wout, g_post):
    b, s, d = x.shape
    tq = QUERY_TILE
    tk = tri.shape[0]
    row = lambda w: pl.BlockSpec((1, tq, w), lambda bi, qi: (bi, qi, 0))
    seq_spec = pl.BlockSpec((1, s, SB_WIDTH), lambda bi, qi: (bi, 0, 0), pipeline_mode=pl.Buffered(1))
    return pl.pallas_call(
        _prompt_attn_kernel,
        grid=(b, s // tq),
        in_specs=[pl.BlockSpec(memory_space=pltpu.SMEM),
                  row(SB_WIDTH), seq_spec, seq_spec, _const_spec(tri.shape),
                  row(d), row(d), row(d),
                  _const_spec(wbr_c.shape), _const_spec(wout.shape), _const_spec(g_post.shape)],
        out_specs=row(d),
        out_shape=jax.ShapeDtypeStruct((b, s, d), F32),
        scratch_shapes=[pltpu.VMEM((SB_HEADS, tq, LANES), BF16),
                        pltpu.VMEM((2, SB_HEADS, tq, tk), F32),
                        pltpu.VMEM((2, SB_HEADS, tq, tk), BF16),
                        pltpu.VMEM((SB_HEADS, tq, tk), BF16),
                        pltpu.VMEM((SB_HEADS, tq, LANES), F32),
                        pltpu.VMEM((SB_HEADS, tq, LANES), F32)],
        compiler_params=pltpu.CompilerParams(dimension_semantics=("arbitrary", "arbitrary"),
                                             vmem_limit_bytes=VMEM_LIMIT),
        name="prompt_attn",
    )(bias, q, kb, vb, tri, mab, gc, x, wbr_c, wout, g_post)


def _ffn_chunks(ffn_dim):
    assert ffn_dim % LANES == 0
    edges = list(range(0, ffn_dim, FFN_CHUNK)) + [ffn_dim]
    return list(zip(edges[:-1], edges[1:]))


def _prompt_ffn_kernel(x_ref, gpre_ref, wup_ref, cw_ref, cb_ref, wdown_ref, gpost_ref,
                       o_ref, conv_ref, ext_ref, *, tiles_per_seq):
    tm = x_ref.shape[0]
    ffn = wdown_ref.shape[0]
    i = pl.program_id(0)

    @pl.when(i % tiles_per_seq == 0)
    def _():
        ext_ref[0:CONV_PREFIX, :] = jnp.zeros((CONV_PREFIX, ext_ref.shape[1]), F32)

    x = x_ref[...]
    h = _rms(x, gpre_ref[...]).astype(BF16)

    def up(lo, hi):
        ext_ref[CONV_PREFIX:, lo:hi] = jnp.dot(h, wup_ref[:, lo:hi], preferred_element_type=F32)

    def conv(lo, hi):
        c = cb_ref[:, lo:hi]
        for j in range(CONV_WIDTH):
            s = CONV_PREFIX - (CONV_WIDTH - 1) + j
            c = c + ext_ref[s:s + tm, lo:hi] * cw_ref[j:j + 1, lo:hi]
        return c

    chunks = _ffn_chunks(ffn)
    f = jnp.zeros((tm, o_ref.shape[1]), F32)
    pending = None
    for i in range(len(chunks) + FFN_LOOKAHEAD + 1):
        if i < len(chunks):
            lo, hi = chunks[i]
            up(lo, hi)
            up(ffn + lo, ffn + hi)
        if pending is not None:
            act, lo, hi = pending
            f = f + jnp.dot(act, wdown_ref[lo:hi, :], preferred_element_type=F32)
            pending = None
        if FFN_LOOKAHEAD <= i < len(chunks) + FFN_LOOKAHEAD:
            lo, hi = chunks[i - FFN_LOOKAHEAD]
            act = _gelu_tanh(conv(lo, hi)) * conv(ffn + lo, ffn + hi)
            pending = (act.astype(BF16), lo, hi)
    o_ref[...] = x + _rms(f, gpost_ref[...])

    last = ext_ref[tm:tm + CONV_PREFIX, :]
    conv_ref[0] = last
    ext_ref[0:CONV_PREFIX, :] = last


def _prompt_ffn(x2, gpre, wup, cw, cb, wdown, gpost, *, batch, seq):
    n, d = x2.shape
    tm = FFN_TOKEN_TILE
    tps = seq // tm
    f2 = wup.shape[1]
    return pl.pallas_call(
        functools.partial(_prompt_ffn_kernel, tiles_per_seq=tps),
        grid=(n // tm,),
        in_specs=[pl.BlockSpec((tm, d), lambda i: (i, 0)),
                  _const_spec(gpre.shape), _const_spec(wup.shape), _const_spec(cw.shape),
                  _const_spec(cb.shape), _const_spec(wdown.shape), _const_spec(gpost.shape)],
        out_specs=(pl.BlockSpec((tm, d), lambda i: (i, 0)),
                   pl.BlockSpec((1, CONV_PREFIX, f2), lambda i: (i // tps, 0, 0))),
        out_shape=(jax.ShapeDtypeStruct((n, d), F32),
                   jax.ShapeDtypeStruct((batch, CONV_PREFIX, f2), F32)),
        scratch_shapes=[pltpu.VMEM((tm + CONV_PREFIX, f2), F32)],
        compiler_params=pltpu.CompilerParams(dimension_semantics=("arbitrary",),
                                             vmem_limit_bytes=VMEM_LIMIT),
        name="prompt_ffn",
    )(x2, gpre, wup, cw, cb, wdown, gpost)


def _sample_in_kernel(x_ref, g_ref, win_ref, pre_ref, poolw_ref, pscale_ref, sgw0_ref, sgb0_ref,
                      wbr_ref, a_ref, sgv_ref, q_ref, k_ref, v_ref, mab_ref, gc_ref, *, pos0):
    h = _rms(x_ref[...], g_ref[...]).astype(BF16)

    def proj(lo, hi):
        return jnp.dot(h, win_ref[:, lo:hi], preferred_element_type=F32)

    a = proj(0, 512)
    a_ref[...] = a
    ya = []
    for gi, w in enumerate(POOL_WINDOWS):
        sl = slice(gi * LANES, (gi + 1) * LANES)
        s = a[:, sl]
        for j in range(1, w):
            s = s + pre_ref[POOL_BUF - j, :, sl]
        d = (s / float(min(pos0 + 1, w)) - a[:, sl]).astype(BF16)
        ya.append(jnp.dot(d, poolw_ref[gi], preferred_element_type=F32) * pscale_ref[:, sl])
    ya = jnp.concatenate(ya, axis=1).astype(BF16)

    u = proj(512, 1024)
    sv = proj(1024, 1536)
    sgv_ref[...] = sv
    yb = (u * (sgw0_ref[...] * sv + sgb0_ref[...])).astype(BF16)

    q_ref[...] = proj(1536, 2048) * SB_SCALE
    k_ref[...] = proj(2048, 2560)
    v_ref[...] = proj(2560, 3072)
    ga = jax.nn.sigmoid(proj(3072, 4096))
    gb = jax.nn.sigmoid(proj(4096, 5120))
    gc_ref[...] = jax.nn.sigmoid(proj(5120, 6144))
    mab_ref[...] = (ga * jnp.dot(ya, wbr_ref[0], preferred_element_type=F32)
                    + gb * jnp.dot(yb, wbr_ref[1], preferred_element_type=F32))


def _sample_in(x, g, win, pre, poolw, pscale, sgw0, sgb0, wbr_ab, *, pos0):
    n, d = x.shape
    sd = lambda w: jax.ShapeDtypeStruct((n, w), F32)
    return pl.pallas_call(
        functools.partial(_sample_in_kernel, pos0=pos0),
        out_shape=(sd(512), sd(512), sd(SB_WIDTH), sd(SB_WIDTH), sd(SB_WIDTH), sd(d), sd(d)),
        compiler_params=pltpu.CompilerParams(vmem_limit_bytes=VMEM_LIMIT),
        name="sample_in",
    )(x, g, win, pre, poolw, pscale, sgw0, sgb0, wbr_ab)


def _sample_attn_kernel(pt_ref, q_ref, bias_ref, t2_ref, *refs, pages_per_step):
    del pt_ref
    k_refs = refs[:pages_per_step]
    v_refs = refs[pages_per_step:2 * pages_per_step]
    o_ref, carry_ref = refs[2 * pages_per_step:]

    @pl.when(pl.program_id(1) == 0)
    def _():
        o_ref[...] = jnp.zeros(o_ref.shape, F32)
        carry_ref[...] = jnp.zeros(carry_ref.shape, F32)

    q8 = q_ref[0].astype(BF16)
    row_k = lax.broadcasted_iota(jnp.int32, (SB_HEADS, KEY_BLOCK), 0)
    col_h = lax.broadcasted_iota(jnp.int32, (SB_HEAD_DIM, SB_HEADS), 1)

    logits = []
    for i in range(pages_per_step):
        z = jnp.zeros((SB_HEADS, KEY_BLOCK), F32)
        for h in range(SB_HEADS):
            res = jnp.dot(q8, k_refs[i][h].astype(BF16), preferred_element_type=F32)
            z = jnp.where(row_k == h, res, z)
        logits.append(z + bias_ref[...])
    softplus = [_softplus(z) for z in logits]
    sums = [jnp.dot(sp.astype(BF16), t2_ref[...], preferred_element_type=F32) for sp in softplus]
    carry = carry_ref[...]
    weights = []
    for z, sp, r in zip(logits, softplus, sums):
        weights.append(jnp.exp(z - sp - r[:, :KEY_BLOCK] - carry).astype(BF16))
        carry = carry + r[:, KEY_BLOCK:]
    carry_ref[...] = carry
    out = o_ref[0]
    for i in range(0, pages_per_step, 2):
        w2 = jnp.concatenate([weights[i], weights[i + 1]], axis=1)
        for h in range(SB_HEADS):
            v2 = jnp.concatenate([v_refs[i][h], v_refs[i + 1][h]], axis=1).astype(BF16)
            res = lax.dot_general(v2, w2, (((1,), (1,)), ((), ())), preferred_element_type=F32)
            out = out + jnp.where(col_h == h, res, 0.0)
    o_ref[0] = out


def _keys_minor(cache):
    return jnp.transpose(cache, (0, 1, 3, 4, 2))


def _sample_attn(page_table, q, bias_col, t2, cache_kt, cache_vt, *, layer):
    n, n_pages = page_table.shape
    assert cache_kt.shape[2:] == (SB_HEADS, SB_HEAD_DIM, KEY_BLOCK)
    pps = SAMPLE_PAGES_PER_STEP
    assert n_pages % pps == 0 and pps % 2 == 0

    def page_spec(i):
        return pl.BlockSpec((None, None, SB_HEADS, SB_HEAD_DIM, KEY_BLOCK),
                            lambda b, s, pt: (layer, pt[b, n_pages - 1 - (s * pps + i)], 0, 0, 0))

    kv_specs = [page_spec(i) for i in range(pps)]
    head_spec = pl.BlockSpec((1, SB_HEADS, SB_HEAD_DIM), lambda b, s, pt: (b, 0, 0))
    out_spec = pl.BlockSpec((1, SB_HEAD_DIM, SB_HEADS), lambda b, s, pt: (b, 0, 0))
    grid_spec = pltpu.PrefetchScalarGridSpec(
        num_scalar_prefetch=1,
        grid=(n, n_pages // pps),
        in_specs=[head_spec,
                  pl.BlockSpec(bias_col.shape, lambda b, s, pt: (0, 0)),
                  pl.BlockSpec(t2.shape, lambda b, s, pt: (0, 0))] + kv_specs + kv_specs,
        out_specs=out_spec,
        scratch_shapes=[pltpu.VMEM((SB_HEADS, KEY_BLOCK), F32)],
    )
    out = pl.pallas_call(
        functools.partial(_sample_attn_kernel, pages_per_step=pps),
        grid_spec=grid_spec,
        out_shape=jax.ShapeDtypeStruct((n, SB_HEAD_DIM, SB_HEADS), F32),
        compiler_params=pltpu.CompilerParams(dimension_semantics=("arbitrary", "arbitrary"),
                                             vmem_limit_bytes=VMEM_LIMIT),
        name="sample_attn",
    )(page_table, q.reshape(n, SB_HEADS, SB_HEAD_DIM), bias_col, t2,
      *([cache_kt] * pps), *([cache_vt] * pps))
    return jnp.transpose(out, (0, 2, 1)).reshape(n, SB_WIDTH)


def _sample_out_kernel(yc_ref, mab_ref, gc_ref, x_ref, wbrc_ref, wout_ref, gpost_ref, gpre_ref,
                       wup_ref, cw_ref, cb_ref, pre_ref, wdown_ref, gfpost_ref, o_ref, up_ref):
    ffn = wdown_ref.shape[0]
    m = mab_ref[...] + gc_ref[...] * jnp.dot(yc_ref[...].astype(BF16), wbrc_ref[...],
                                             preferred_element_type=F32)
    mixed = jnp.dot(m.astype(BF16), wout_ref[...], preferred_element_type=F32)
    x = x_ref[...] + _rms(mixed, gpost_ref[...])
    h = _rms(x, gpre_ref[...]).astype(BF16)

    def conv(lo, hi):
        up = jnp.dot(h, wup_ref[:, lo:hi], preferred_element_type=F32)
        up_ref[:, lo:hi] = up
        c = cb_ref[:, lo:hi] + up * cw_ref[CONV_WIDTH - 1:CONV_WIDTH, lo:hi]
        for j in range(CONV_WIDTH - 1):
            c = c + pre_ref[j, :, lo:hi] * cw_ref[j:j + 1, lo:hi]
        return c

    f = jnp.zeros(x.shape, F32)
    for lo, hi in _ffn_chunks(ffn):
        act = _gelu_tanh(conv(lo, hi)) * conv(ffn + lo, ffn + hi)
        f = f + jnp.dot(act.astype(BF16), wdown_ref[lo:hi, :], preferred_element_type=F32)
    o_ref[...] = x + _rms(f, gfpost_ref[...])


def _sample_out(yc, mab, gc, x, wbr_c, wout, gpost, gpre, wup, cw, cb, pre, wdown, gfpost):
    n, d = x.shape
    return pl.pallas_call(
        _sample_out_kernel,
        out_shape=(jax.ShapeDtypeStruct((n, d), F32),
                   jax.ShapeDtypeStruct((n, wup.shape[1]), F32)),
        compiler_params=pltpu.CompilerParams(vmem_limit_bytes=VMEM_LIMIT),
        name="sample_out",
    )(yc, mab, gc, x, wbr_c, wout, gpost, gpre, wup, cw, cb, pre, wdown, gfpost)


def kernel(x_prompt, x_sample, state_pool, cache_k, cache_v, page_table, state_conv,
           norm_mix_pre, norm_mix_post, norm_ffn_pre, norm_ffn_post, w_in, pool_w,
           pool_scale, sg_w, sg_b, sb_bias, w_branch, w_out, w_up, conv_w, conv_b, w_down):
    depth = w_in.shape[0]
    batch, seq, d = x_prompt.shape
    dec = x_sample.shape[0]
    page = cache_k.shape[2]
    past_len = page_table.shape[1] * page
    assert x_sample.shape[1] == 1 and past_len % CHUNK == 0 and page == KEY_BLOCK
    assert seq % TOKEN_TILE == 0 and seq % FFN_TOKEN_TILE == 0 and seq % QUERY_TILE == 0

    tri = _lower(QUERY_TILE)
    t2 = _suffix_sum_matrix(KEY_BLOCK)
    cache_kt = _keys_minor(cache_k)
    cache_vt = _keys_minor(cache_v)
    pool_pre = jnp.transpose(state_pool, (0, 2, 1, 3))
    conv_pre = jnp.transpose(state_conv, (0, 2, 1, 3))

    xp = x_prompt.reshape(batch * seq, d)
    xs = x_sample.reshape(dec, d)
    outs = [[] for _ in range(9)]
    for l in range(depth):
        row = lambda w: w[l][None, :]
        win = w_in[l].astype(BF16)
        poolw = pool_w[l].astype(BF16)
        wbr = w_branch[l].astype(BF16)
        wout = w_out[l].astype(BF16)
        wup = w_up[l].astype(BF16)
        wdown = w_down[l].astype(BF16)
        pscale = row(pool_scale)
        cb = row(conv_b)
        g_pre, g_post = row(norm_mix_pre), row(norm_mix_post)
        gf_pre, gf_post = row(norm_ffn_pre), row(norm_ffn_post)

        q, k, v, kb, vb, mab, gc, pool = _prompt_in(
            xp, g_pre, win, poolw, pscale, sg_w[l], sg_b[l].T, wbr[:2], batch=batch, seq=seq)
        sh = lambda t: t.reshape(batch, seq, t.shape[-1])
        x1 = _prompt_attn(sb_bias[l], sh(q), sh(kb), sh(vb), tri, sh(mab), sh(gc), sh(xp),
                          wbr[2], wout, g_post)
        xp, conv = _prompt_ffn(x1.reshape(batch * seq, d), gf_pre, wup, conv_w[l], cb, wdown,
                               gf_post, batch=batch, seq=seq)

        sgw0 = jnp.repeat(sg_w[l][:, 0, 0], LANES)[None, :]
        sgb0 = jnp.repeat(sg_b[l][:, 0], LANES)[None, :]
        a_s, sgv, q_s, k_s, v_s, mab_s, gc_s = _sample_in(
            xs, g_pre, win, pool_pre[l], poolw, pscale, sgw0, sgb0, wbr[:2], pos0=past_len)
        yc_s = _sample_attn(page_table, q_s, sb_bias[l][:, None], t2, cache_kt, cache_vt, layer=l)
        xs, up_s = _sample_out(yc_s, mab_s, gc_s, xs, wbr[2], wout, g_post,
                               gf_pre, wup, conv_w[l], cb, conv_pre[l], wdown, gf_post)

        outs[0].append(pool[:, 1:])
        outs[1].append(jnp.concatenate([state_pool[l][:, 1:], a_s[:, None]], axis=1))
        heads_last = lambda t: jnp.transpose(t.reshape(batch, SB_HEADS, SB_HEAD_DIM, seq), (0, 3, 1, 2))
        outs[2].append(heads_last(k))
        outs[3].append(heads_last(v))
        outs[4].append(k_s.reshape(dec, 1, SB_HEADS, SB_HEAD_DIM))
        outs[5].append(v_s.reshape(dec, 1, SB_HEADS, SB_HEAD_DIM))
        outs[6].append(sgv[:, None])
        outs[7].append(conv[:, CONV_PREFIX - (CONV_WIDTH - 1):])
        outs[8].append(jnp.concatenate([state_conv[l][:, 1:], up_s[:, None]], axis=1))

    return (xp.reshape(batch, seq, d), xs.reshape(dec, 1, d)) + tuple(jnp.stack(o) for o in outs)
```

```python
import functools

import jax
import jax.numpy as jnp
from jax import lax
from jax.experimental import pallas as pl
from jax.experimental.pallas import tpu as pltpu

F32 = jnp.float32
BF16 = jnp.bfloat16

LANES = 128
POOL_WINDOWS = (2, 4, 8, 16)
POOL_PREFIX = max(POOL_WINDOWS)
POOL_BUF = POOL_PREFIX - 1
CHUNK = 128
SB_HEADS = 8
SB_HEAD_DIM = 64
SB_WIDTH = SB_HEADS * SB_HEAD_DIM
SB_SCALE = SB_HEAD_DIM ** -0.5
KEY_BLOCK = 128
CONV_WIDTH = 3
CONV_PREFIX = 8
EPS = 1e-6
LOG2E = 1.4426950408889634
VMEM_LIMIT = 56 * 1024 * 1024

TOKEN_TILE = 512
FFN_TOKEN_TILE = 256
QUERY_TILE = 256
FFN_CHUNK = 256
FFN_LOOKAHEAD = 3
SAMPLE_PAGES_PER_STEP = 16


def _rms(x, g):
    return x * lax.rsqrt(jnp.mean(x * x, axis=-1, keepdims=True) + EPS) * g


def _softplus(z):
    return jnp.maximum(z, 0.0) + jnp.log(1.0 + jnp.exp2(jnp.abs(z) * -LOG2E))


def _gelu_tanh(x):
    return 0.5 * x * (1.0 + jnp.tanh(0.7978845608028654 * (x + 0.044715 * (x * x * x))))


def _const_spec(shape):
    nd = len(shape)
    return pl.BlockSpec(shape, lambda *_: (0,) * nd, pipeline_mode=pl.Buffered(1))


def _suffix_sum_matrix(n):
    return jnp.concatenate([_strict_lower(n), jnp.ones((n, n), BF16)], axis=1)


def _strict_lower(n):
    r = lax.broadcasted_iota(jnp.int32, (n, n), 0)
    c = lax.broadcasted_iota(jnp.int32, (n, n), 1)
    return (r > c).astype(BF16)


def _lower(n):
    r = lax.broadcasted_iota(jnp.int32, (n, n), 0)
    c = lax.broadcasted_iota(jnp.int32, (n, n), 1)
    return (r >= c).astype(BF16)


def _prompt_in_kernel(x_ref, g_ref, win_ref, poolw_ref, pscale_ref, sgw_ref, sgb_ref, wbr_ref,
                      q_ref, k_ref, v_ref, kb_ref, vb_ref, mab_ref, gc_ref, pool_ref,
                      ext_ref, *, tiles_per_seq):
    tm = x_ref.shape[0]
    i = pl.program_id(0)
    t0 = (i % tiles_per_seq) * tm

    @pl.when(i % tiles_per_seq == 0)
    def _():
        ext_ref[0:POOL_PREFIX, :] = jnp.zeros((POOL_PREFIX, ext_ref.shape[1]), F32)

    h = _rms(x_ref[...], g_ref[...]).astype(BF16)

    def proj(lo, hi):
        return jnp.dot(h, win_ref[:, lo:hi], preferred_element_type=F32)

    ext_ref[POOL_PREFIX:, :] = proj(0, 512)
    pos1 = (t0 + 1 + lax.broadcasted_iota(jnp.int32, (tm, LANES), 0)).astype(F32)
    ya = []
    for gi, w in enumerate(POOL_WINDOWS):
        sl = slice(gi * LANES, (gi + 1) * LANES)
        cur = ext_ref[POOL_PREFIX:POOL_PREFIX + tm, sl]
        s = cur
        for j in range(1, w):
            s = s + ext_ref[POOL_PREFIX - j:POOL_PREFIX - j + tm, sl]
        d = (s / jnp.minimum(pos1, float(w)) - cur).astype(BF16)
        ya.append(jnp.dot(d, poolw_ref[gi], preferred_element_type=F32) * pscale_ref[:, sl])
    ya = jnp.concatenate(ya, axis=1).astype(BF16)
    last = ext_ref[tm:tm + POOL_PREFIX, :]
    pool_ref[0] = last
    ext_ref[0:POOL_PREFIX, :] = last

    u = proj(512, 1024)
    sv = proj(1024, 1536).astype(BF16)
    r = lax.broadcasted_iota(jnp.int32, (CHUNK, CHUNK), 0)
    c = lax.broadcasted_iota(jnp.int32, (CHUNK, CHUNK), 1)
    cols = []
    for gi in range(4):
        sl = slice(gi * LANES, (gi + 1) * LANES)
        wg = jnp.where(r >= c, sgw_ref[gi], 0.0).astype(BF16)
        bcol = sgb_ref[:, gi:gi + 1]
        rows = []
        for ci in range(tm // CHUNK):
            rs = slice(ci * CHUNK, (ci + 1) * CHUNK)
            mixed = jnp.dot(wg, sv[rs, sl], preferred_element_type=F32) + bcol
            rows.append(u[rs, sl] * mixed)
        cols.append(jnp.concatenate(rows, axis=0))
    yb = jnp.concatenate(cols, axis=1).astype(BF16)

    q_ref[...] = (proj(1536, 2048) * SB_SCALE).astype(BF16)
    k = proj(2048, 2560)
    k_ref[0] = k.T
    kb_ref[...] = k.astype(BF16)
    v = proj(2560, 3072)
    v_ref[0] = v.T
    vb_ref[...] = v.astype(BF16)

    ga = jax.nn.sigmoid(proj(3072, 4096))
    gb = jax.nn.sigmoid(proj(4096, 5120))
    gc_ref[...] = jax.nn.sigmoid(proj(5120, 6144)).astype(BF16)
    mab_ref[...] = (ga * jnp.dot(ya, wbr_ref[0], preferred_element_type=F32)
                    + gb * jnp.dot(yb, wbr_ref[1], preferred_element_type=F32))


def _prompt_in(x2, g, win, poolw, pscale, sgw, sgb_t, wbr_ab, *, batch, seq):
    n, d = x2.shape
    tm = TOKEN_TILE
    tps = seq // tm
    row = lambda w: pl.BlockSpec((tm, w), lambda i: (i, 0))
    out_shape = (
        jax.ShapeDtypeStruct((n, SB_WIDTH), BF16),
        jax.ShapeDtypeStruct((batch, SB_WIDTH, seq), F32),
        jax.ShapeDtypeStruct((batch, SB_WIDTH, seq), F32),
        jax.ShapeDtypeStruct((n, SB_WIDTH), BF16),
        jax.ShapeDtypeStruct((n, SB_WIDTH), BF16),
        jax.ShapeDtypeStruct((n, d), F32),
        jax.ShapeDtypeStruct((n, d), BF16),
        jax.ShapeDtypeStruct((batch, POOL_PREFIX, 512), F32),
    )
    col = pl.BlockSpec((1, SB_WIDTH, tm), lambda i: (i // tps, 0, i % tps))
    out_specs = (row(SB_WIDTH), col, col, row(SB_WIDTH), row(SB_WIDTH),
                 row(d), row(d),
                 pl.BlockSpec((1, POOL_PREFIX, 512), lambda i: (i // tps, 0, 0)))
    return pl.pallas_call(
        functools.partial(_prompt_in_kernel, tiles_per_seq=tps),
        grid=(n // tm,),
        in_specs=[row(d), _const_spec(g.shape), _const_spec(win.shape), _const_spec(poolw.shape),
                  _const_spec(pscale.shape), _const_spec(sgw.shape), _const_spec(sgb_t.shape),
                  _const_spec(wbr_ab.shape)],
        out_specs=out_specs,
        out_shape=out_shape,
        scratch_shapes=[pltpu.VMEM((tm + POOL_PREFIX, 512), F32)],
        compiler_params=pltpu.CompilerParams(dimension_semantics=("arbitrary",),
                                             vmem_limit_bytes=VMEM_LIMIT),
        name="prompt_in",
    )(x2, g, win, poolw, pscale, sgw, sgb_t, wbr_ab)


def _prompt_attn_kernel(bias_ref, q_ref, kb_ref, vb_ref, tri_ref, mab_ref, gc_ref, x_ref,
                        wbrc_ref, wout_ref, g_ref, o_ref,
                        qm_ref, z_ref, spb_ref, ab_ref, acc_ref, carry_ref):
    tq = q_ref.shape[1]
    tk = tri_ref.shape[0]
    q0 = pl.program_id(1) * tq

    lane = lax.broadcasted_iota(jnp.int32, (tq, LANES), 1)
    low = lane < SB_HEAD_DIM
    for p in range(SB_HEADS // 2):
        qt = q_ref[0, :, p * LANES:(p + 1) * LANES]
        qm_ref[2 * p] = jnp.where(low, qt, 0.0).astype(BF16)
        qm_ref[2 * p + 1] = jnp.where(low, 0.0, qt).astype(BF16)
    acc_ref[...] = jnp.zeros(acc_ref.shape, F32)
    carry_ref[...] = jnp.zeros(carry_ref.shape, F32)

    def tile(ref, k0, h):
        return ref[0, pl.ds(k0, tk), (h // 2) * LANES:(h // 2 + 1) * LANES]

    def logits(k0, slot, h, mask):
        z = lax.dot_general(qm_ref[h], tile(kb_ref, k0, h), (((1,), (1,)), ((), ())),
                            preferred_element_type=F32) + bias_ref[h]
        sp = _softplus(z)
        if mask is not None:
            sp = jnp.where(mask, sp, 0.0)
            z = jnp.where(mask, z, -jnp.inf)
        z_ref[slot, h] = z
        spb_ref[slot, h] = sp.astype(BF16)

    def suffix(slot, h):
        return jnp.dot(spb_ref[slot, h], tri_ref[...], preferred_element_type=F32)

    def weights(slot, h, c):
        cr = carry_ref[h]
        a = jnp.exp(z_ref[slot, h] - c - jnp.concatenate([cr] * (tk // LANES), axis=1))
        ab_ref[h] = a.astype(BF16)
        carry_ref[h] = cr + jnp.broadcast_to(c[:, 0:1], cr.shape)

    def accumulate(k0, h):
        acc_ref[h] += jnp.dot(ab_ref[h], tile(vb_ref, k0, h), preferred_element_type=F32)

    def stage(apply_block, logits_block):
        c_prev = None
        for h in range(SB_HEADS + 1):
            if apply_block is not None:
                c = suffix(apply_block[1], h) if h < SB_HEADS else None
                if h >= 1:
                    weights(apply_block[1], h - 1, c_prev)
                    accumulate(apply_block[0], h - 1)
                c_prev = c
            if logits_block is not None and h < SB_HEADS:
                logits(logits_block[0], logits_block[1], h, None)

    def k_start(j):
        return pl.multiple_of(q0 - j * tk, tk)

    assert tq == tk
    n_apply = q0 // tk
    causal = (lax.broadcasted_iota(jnp.int32, (tq, tk), 1)
              < lax.broadcasted_iota(jnp.int32, (tq, tk), 0))
    for h in range(SB_HEADS):
        logits(k_start(0), 0, h, causal)

    def body(i, _):
        j = 2 * i
        stage((k_start(j), 0), (k_start(j + 1), 1))
        stage((k_start(j + 1), 1), (k_start(j + 2), 0))
        return 0

    lax.fori_loop(0, n_apply // 2, body, 0)

    @pl.when(n_apply % 2 == 1)
    def _():
        stage((k_start(n_apply - 1), 0), (k_start(n_apply), 1))
        stage((k_start(n_apply), 1), None)

    @pl.when(n_apply % 2 == 0)
    def _():
        stage((k_start(n_apply), 0), None)

    tiles = []
    for p in range(SB_HEADS // 2):
        tiles.append(jnp.where(low, acc_ref[2 * p], acc_ref[2 * p + 1]))
    yc = jnp.concatenate(tiles, axis=1).astype(BF16)
    m = mab_ref[0] + gc_ref[0].astype(F32) * jnp.dot(yc, wbrc_ref[...], preferred_element_type=F32)
    mixed = jnp.dot(m.astype(BF16), wout_ref[...], preferred_element_type=F32)
    o_ref[0] = x_ref[0] + _rms(mixed, g_ref[...])


def _prompt_attn(bias, q, kb, vb, tri, mab, gc, x, wbr_c, wout, g_post):
    b, s, d = x.shape
    tq = QUERY_TILE
    tk = tri.shape[0]
    row = lambda w: pl.BlockSpec((1, tq, w), lambda bi, qi: (bi, qi, 0))
    seq_spec = pl.BlockSpec((1, s, SB_WIDTH), lambda bi, qi: (bi, 0, 0), pipeline_mode=pl.Buffered(1))
    return pl.pallas_call(
        _prompt_attn_kernel,
        grid=(b, s // tq),
        in_specs=[pl.BlockSpec(memory_space=pltpu.SMEM),
                  row(SB_WIDTH), seq_spec, seq_spec, _const_spec(tri.shape),
                  row(d), row(d), row(d),
                  _const_spec(wbr_c.shape), _const_spec(wout.shape), _const_spec(g_post.shape)],
        out_specs=row(d),
        out_shape=jax.ShapeDtypeStruct((b, s, d), F32),
        scratch_shapes=[pltpu.VMEM((SB_HEADS, tq, LANES), BF16),
                        pltpu.VMEM((2, SB_HEADS, tq, tk), F32),
                        pltpu.VMEM((2, SB_HEADS, tq, tk), BF16),
                        pltpu.VMEM((SB_HEADS, tq, tk), BF16),
                        pltpu.VMEM((SB_HEADS, tq, LANES), F32),
                        pltpu.VMEM((SB_HEADS, tq, LANES), F32)],
        compiler_params=pltpu.CompilerParams(dimension_semantics=("arbitrary", "arbitrary"),
                                             vmem_limit_bytes=VMEM_LIMIT),
        name="prompt_attn",
    )(bias, q, kb, vb, tri, mab, gc, x, wbr_c, wout, g_post)


def _ffn_chunks(ffn_dim):
    assert ffn_dim % LANES == 0
    edges = list(range(0, ffn_dim, FFN_CHUNK)) + [ffn_dim]
    return list(zip(edges[:-1], edges[1:]))


def _prompt_ffn_kernel(x_ref, gpre_ref, wup_ref, cw_ref, cb_ref, wdown_ref, gpost_ref,
                       o_ref, conv_ref, ext_ref, *, tiles_per_seq):
    tm = x_ref.shape[0]
    ffn = wdown_ref.shape[0]
    i = pl.program_id(0)

    @pl.when(i % tiles_per_seq == 0)
    def _():
        ext_ref[0:CONV_PREFIX, :] = jnp.zeros((CONV_PREFIX, ext_ref.shape[1]), F32)

    x = x_ref[...]
    h = _rms(x, gpre_ref[...]).astype(BF16)

    def up(lo, hi):
        ext_ref[CONV_PREFIX:, lo:hi] = jnp.dot(h, wup_ref[:, lo:hi], preferred_element_type=F32)

    def conv(lo, hi):
        c = cb_ref[:, lo:hi]
        for j in range(CONV_WIDTH):
            s = CONV_PREFIX - (CONV_WIDTH - 1) + j
            c = c + ext_ref[s:s + tm, lo:hi] * cw_ref[j:j + 1, lo:hi]
        return c

    chunks = _ffn_chunks(ffn)
    f = jnp.zeros((tm, o_ref.shape[1]), F32)
    pending = None
    for i in range(len(chunks) + FFN_LOOKAHEAD + 1):
        if i < len(chunks):
            lo, hi = chunks[i]
            up(lo, hi)
            up(ffn + lo, ffn + hi)
        if pending is not None:
            act, lo, hi = pending
            f = f + jnp.dot(act, wdown_ref[lo:hi, :], preferred_element_type=F32)
            pending = None
        if FFN_LOOKAHEAD <= i < len(chunks) + FFN_LOOKAHEAD:
            lo, hi = chunks[i - FFN_LOOKAHEAD]
            act = _gelu_tanh(conv(lo, hi)) * conv(ffn + lo, ffn + hi)
            pending = (act.astype(BF16), lo, hi)
    o_ref[...] = x + _rms(f, gpost_ref[...])

    last = ext_ref[tm:tm + CONV_PREFIX, :]
    conv_ref[0] = last
    ext_ref[0:CONV_PREFIX, :] = last


def _prompt_ffn(x2, gpre, wup, cw, cb, wdown, gpost, *, batch, seq):
    n, d = x2.shape
    tm = FFN_TOKEN_TILE
    tps = seq // tm
    f2 = wup.shape[1]
    return pl.pallas_call(
        functools.partial(_prompt_ffn_kernel, tiles_per_seq=tps),
        grid=(n // tm,),
        in_specs=[pl.BlockSpec((tm, d), lambda i: (i, 0)),
                  _const_spec(gpre.shape), _const_spec(wup.shape), _const_spec(cw.shape),
                  _const_spec(cb.shape), _const_spec(wdown.shape), _const_spec(gpost.shape)],
        out_specs=(pl.BlockSpec((tm, d), lambda i: (i, 0)),
                   pl.BlockSpec((1, CONV_PREFIX, f2), lambda i: (i // tps, 0, 0))),
        out_shape=(jax.ShapeDtypeStruct((n, d), F32),
                   jax.ShapeDtypeStruct((batch, CONV_PREFIX, f2), F32)),
        scratch_shapes=[pltpu.VMEM((tm + CONV_PREFIX, f2), F32)],
        compiler_params=pltpu.CompilerParams(dimension_semantics=("arbitrary",),
                                             vmem_limit_bytes=VMEM_LIMIT),
        name="prompt_ffn",
    )(x2, gpre, wup, cw, cb, wdown, gpost)


def _sample_in_kernel(x_ref, g_ref, win_ref, pre_ref, poolw_ref, pscale_ref, sgw0_ref, sgb0_ref,
                      wbr_ref, a_ref, sgv_ref, q_ref, k_ref, v_ref, mab_ref, gc_ref, *, pos0):
    h = _rms(x_ref[...], g_ref[...]).astype(BF16)

    def proj(lo, hi):
        return jnp.dot(h, win_ref[:, lo:hi], preferred_element_type=F32)

    a = proj(0, 512)
    a_ref[...] = a
    ya = []
    for gi, w in enumerate(POOL_WINDOWS):
        sl = slice(gi * LANES, (gi + 1) * LANES)
        s = a[:, sl]
        for j in range(1, w):
            s = s + pre_ref[POOL_BUF - j, :, sl]
        d = (s / float(min(pos0 + 1, w)) - a[:, sl]).astype(BF16)
        ya.append(jnp.dot(d, poolw_ref[gi], preferred_element_type=F32) * pscale_ref[:, sl])
    ya = jnp.concatenate(ya, axis=1).astype(BF16)

    u = proj(512, 1024)
    sv = proj(1024, 1536)
    sgv_ref[...] = sv
    yb = (u * (sgw0_ref[...] * sv + sgb0_ref[...])).astype(BF16)

    q_ref[...] = proj(1536, 2048) * SB_SCALE
    k_ref[...] = proj(2048, 2560)
    v_ref[...] = proj(2560, 3072)
    ga = jax.nn.sigmoid(proj(3072, 4096))
    gb = jax.nn.sigmoid(proj(4096, 5120))
    gc_ref[...] = jax.nn.sigmoid(proj(5120, 6144))
    mab_ref[...] = (ga * jnp.dot(ya, wbr_ref[0], preferred_element_type=F32)
                    + gb * jnp.dot(yb, wbr_ref[1], preferred_element_type=F32))


def _sample_in(x, g, win, pre, poolw, pscale, sgw0, sgb0, wbr_ab, *, pos0):
    n, d = x.shape
    sd = lambda w: jax.ShapeDtypeStruct((n, w), F32)
    return pl.pallas_call(
        functools.partial(_sample_in_kernel, pos0=pos0),
        out_shape=(sd(512), sd(512), sd(SB_WIDTH), sd(SB_WIDTH), sd(SB_WIDTH), sd(d), sd(d)),
        compiler_params=pltpu.CompilerParams(vmem_limit_bytes=VMEM_LIMIT),
        name="sample_in",
    )(x, g, win, pre, poolw, pscale, sgw0, sgb0, wbr_ab)


def _sample_attn_kernel(pt_ref, q_ref, bias_ref, t2_ref, *refs, pages_per_step):
    del pt_ref
    k_refs = refs[:pages_per_step]
    v_refs = refs[pages_per_step:2 * pages_per_step]
    o_ref, carry_ref = refs[2 * pages_per_step:]

    @pl.when(pl.program_id(1) == 0)
    def _():
        o_ref[...] = jnp.zeros(o_ref.shape, F32)
        carry_ref[...] = jnp.zeros(carry_ref.shape, F32)

    q8 = q_ref[0].astype(BF16)
    row_k = lax.broadcasted_iota(jnp.int32, (SB_HEADS, KEY_BLOCK), 0)
    col_h = lax.broadcasted_iota(jnp.int32, (SB_HEAD_DIM, SB_HEADS), 1)

    logits = []
    for i in range(pages_per_step):
        z = jnp.zeros((SB_HEADS, KEY_BLOCK), F32)
        for h in range(SB_HEADS):
            res = jnp.dot(q8, k_refs[i][h].astype(BF16), preferred_element_type=F32)
            z = jnp.where(row_k == h, res, z)
        logits.append(z + bias_ref[...])
    softplus = [_softplus(z) for z in logits]
    sums = [jnp.dot(sp.astype(BF16), t2_ref[...], preferred_element_type=F32) for sp in softplus]
    carry = carry_ref[...]
    weights = []
    for z, sp, r in zip(logits, softplus, sums):
        weights.append(jnp.exp(z - sp - r[:, :KEY_BLOCK] - carry).astype(BF16))
        carry = carry + r[:, KEY_BLOCK:]
    carry_ref[...] = carry
    out = o_ref[0]
    for i in range(0, pages_per_step, 2):
        w2 = jnp.concatenate([weights[i], weights[i + 1]], axis=1)
        for h in range(SB_HEADS):
            v2 = jnp.concatenate([v_refs[i][h], v_refs[i + 1][h]], axis=1).astype(BF16)
            res = lax.dot_general(v2, w2, (((1,), (1,)), ((), ())), preferred_element_type=F32)
            out = out + jnp.where(col_h == h, res, 0.0)
    o_ref[0] = out


def _keys_minor(cache):
    return jnp.transpose(cache, (0, 1, 3, 4, 2))


def _sample_attn(page_table, q, bias_col, t2, cache_kt, cache_vt, *, layer):
    n, n_pages = page_table.shape
    assert cache_kt.shape[2:] == (SB_HEADS, SB_HEAD_DIM, KEY_BLOCK)
    pps = SAMPLE_PAGES_PER_STEP
    assert n_pages % pps == 0 and pps % 2 == 0

    def page_spec(i):
        return pl.BlockSpec((None, None, SB_HEADS, SB_HEAD_DIM, KEY_BLOCK),
                            lambda b, s, pt: (layer, pt[b, n_pages - 1 - (s * pps + i)], 0, 0, 0))

    kv_specs = [page_spec(i) for i in range(pps)]
    head_spec = pl.BlockSpec((1, SB_HEADS, SB_HEAD_DIM), lambda b, s, pt: (b, 0, 0))
    out_spec = pl.BlockSpec((1, SB_HEAD_DIM, SB_HEADS), lambda b, s, pt: (b, 0, 0))
    grid_spec = pltpu.PrefetchScalarGridSpec(
        num_scalar_prefetch=1,
        grid=(n, n_pages // pps),
        in_specs=[head_spec,
                  pl.BlockSpec(bias_col.shape, lambda b, s, pt: (0, 0)),
                  pl.BlockSpec(t2.shape, lambda b, s, pt: (0, 0))] + kv_specs + kv_specs,
        out_specs=out_spec,
        scratch_shapes=[pltpu.VMEM((SB_HEADS, KEY_BLOCK), F32)],
    )
    out = pl.pallas_call(
        functools.partial(_sample_attn_kernel, pages_per_step=pps),
        grid_spec=grid_spec,
        out_shape=jax.ShapeDtypeStruct((n, SB_HEAD_DIM, SB_HEADS), F32),
        compiler_params=pltpu.CompilerParams(dimension_semantics=("arbitrary", "arbitrary"),
                                             vmem_limit_bytes=VMEM_LIMIT),
        name="sample_attn",
    )(page_table, q.reshape(n, SB_HEADS, SB_HEAD_DIM), bias_col, t2,
      *([cache_kt] * pps), *([cache_vt] * pps))
    return jnp.transpose(out, (0, 2, 1)).reshape(n, SB_WIDTH)


def _sample_out_kernel(yc_ref, mab_ref, gc_ref, x_ref, wbrc_ref, wout_ref, gpost_ref, gpre_ref,
                       wup_ref, cw_ref, cb_ref, pre_ref, wdown_ref, gfpost_ref, o_ref, up_ref):
    ffn = wdown_ref.shape[0]
    m = mab_ref[...] + gc_ref[...] * jnp.dot(yc_ref[...].astype(BF16), wbrc_ref[...],
                                             preferred_element_type=F32)
    mixed = jnp.dot(m.astype(BF16), wout_ref[...], preferred_element_type=F32)
    x = x_ref[...] + _rms(mixed, gpost_ref[...])
    h = _rms(x, gpre_ref[...]).astype(BF16)

    def conv(lo, hi):
        up = jnp.dot(h, wup_ref[:, lo:hi], preferred_element_type=F32)
        up_ref[:, lo:hi] = up
        c = cb_ref[:, lo:hi] + up * cw_ref[CONV_WIDTH - 1:CONV_WIDTH, lo:hi]
        for j in range(CONV_WIDTH - 1):
            c = c + pre_ref[j, :, lo:hi] * cw_ref[j:j + 1, lo:hi]
        return c

    f = jnp.zeros(x.shape, F32)
    for lo, hi in _ffn_chunks(ffn):
        act = _gelu_tanh(conv(lo, hi)) * conv(ffn + lo, ffn + hi)
        f = f + jnp.dot(act.astype(BF16), wdown_ref[lo:hi, :], preferred_element_type=F32)
    o_ref[...] = x + _rms(f, gfpost_ref[...])


def _sample_out(yc, mab, gc, x, wbr_c, wout, gpost, gpre, wup, cw, cb, pre, wdown, gfpost):
    n, d = x.shape
    return pl.pallas_call(
        _sample_out_kernel,
        out_shape=(jax.ShapeDtypeStruct((n, d), F32),
                   jax.ShapeDtypeStruct((n, wup.shape[1]), F32)),
        compiler_params=pltpu.CompilerParams(vmem_limit_bytes=VMEM_LIMIT),
        name="sample_out",
    )(yc, mab, gc, x, wbr_c, wout, gpost, gpre, wup, cw, cb, pre, wdown, gfpost)


def kernel(x_prompt, x_sample, state_pool, cache_k, cache_v, page_table, state_conv,
           norm_mix_pre, norm_mix_post, norm_ffn_pre, norm_ffn_post, w_in, pool_w,
           pool_scale, sg_w, sg_b, sb_bias, w_branch, w_out, w_up, conv_w, conv_b, w_down):
    depth = w_in.shape[0]
    batch, seq, d = x_prompt.shape
    dec = x_sample.shape[0]
    page = cache_k.shape[2]
    past_len = page_table.shape[1] * page
    assert x_sample.shape[1] == 1 and past_len % CHUNK == 0 and page == KEY_BLOCK
    assert seq % TOKEN_TILE == 0 and seq % FFN_TOKEN_TILE == 0 and seq % QUERY_TILE == 0

    tri = _lower(QUERY_TILE)
    t2 = _suffix_sum_matrix(KEY_BLOCK)
    cache_kt = _keys_minor(cache_k)
    cache_vt = _keys_minor(cache_v)
    pool_pre = jnp.transpose(state_pool, (0, 2, 1, 3))
    conv_pre = jnp.transpose(state_conv, (0, 2, 1, 3))

    xp = x_prompt.reshape(batch * seq, d)
    xs = x_sample.reshape(dec, d)
    outs = [[] for _ in range(9)]
    for l in range(depth):
        row = lambda w: w[l][None, :]
        win = w_in[l].astype(BF16)
        poolw = pool_w[l].astype(BF16)
        wbr = w_branch[l].astype(BF16)
        wout = w_out[l].astype(BF16)
        wup = w_up[l].astype(BF16)
        wdown = w_down[l].astype(BF16)
        pscale = row(pool_scale)
        cb = row(conv_b)
        g_pre, g_post = row(norm_mix_pre), row(norm_mix_post)
        gf_pre, gf_post = row(norm_ffn_pre), row(norm_ffn_post)

        q, k, v, kb, vb, mab, gc, pool = _prompt_in(
            xp, g_pre, win, poolw, pscale, sg_w[l], sg_b[l].T, wbr[:2], batch=batch, seq=seq)
        sh = lambda t: t.reshape(batch, seq, t.shape[-1])
        x1 = _prompt_attn(sb_bias[l], sh(q), sh(kb), sh(vb), tri, sh(mab), sh(gc), sh(xp),
                          wbr[2], wout, g_post)
        xp, conv = _prompt_ffn(x1.reshape(batch * seq, d), gf_pre, wup, conv_w[l], cb, wdown,
                               gf_post, batch=batch, seq=seq)

        sgw0 = jnp.repeat(sg_w[l][:, 0, 0], LANES)[None, :]
        sgb0 = jnp.repeat(sg_b[l][:, 0], LANES)[None, :]
        a_s, sgv, q_s, k_s, v_s, mab_s, gc_s = _sample_in(
            xs, g_pre, win, pool_pre[l], poolw, pscale, sgw0, sgb0, wbr[:2], pos0=past_len)
        yc_s = _sample_attn(page_table, q_s, sb_bias[l][:, None], t2, cache_kt, cache_vt, layer=l)
        xs, up_s = _sample_out(yc_s, mab_s, gc_s, xs, wbr[2], wout, g_post,
                               gf_pre, wup, conv_w[l], cb, conv_pre[l], wdown, gf_post)

        outs[0].append(pool[:, 1:])
        outs[1].append(jnp.concatenate([state_pool[l][:, 1:], a_s[:, None]], axis=1))
        heads_last = lambda t: jnp.transpose(t.reshape(batch, SB_HEADS, SB_HEAD_DIM, seq), (0, 3, 1, 2))
        outs[2].append(heads_last(k))
        outs[3].append(heads_last(v))
        outs[4].append(k_s.reshape(dec, 1, SB_HEADS, SB_HEAD_DIM))
        outs[5].append(v_s.reshape(dec, 1, SB_HEADS, SB_HEAD_DIM))
        outs[6].append(sgv[:, None])
        outs[7].append(conv[:, CONV_PREFIX - (CONV_WIDTH - 1):])
        outs[8].append(jnp.concatenate([state_conv[l][:, 1:], up_s[:, None]], axis=1))

    return (xp.reshape(batch, seq, d), xs.reshape(dec, 1, d)) + tuple(jnp.stack(o) for o in outs)
```

```python
import functools

import jax
import jax.numpy as jnp
from jax import lax
from jax.experimental import pallas as pl
from jax.experimental.pallas import tpu as pltpu

F32 = jnp.float32
BF16 = jnp.bfloat16

LANES = 128
POOL_WINDOWS = (2, 4, 8, 16)
POOL_PREFIX = max(POOL_WINDOWS)
POOL_BUF = POOL_PREFIX - 1
CHUNK = 128
SB_HEADS = 8
SB_HEAD_DIM = 64
SB_WIDTH = SB_HEADS * SB_HEAD_DIM
SB_SCALE = SB_HEAD_DIM ** -0.5
KEY_BLOCK = 128
CONV_WIDTH = 3
CONV_PREFIX = 8
EPS = 1e-6
LOG2E = 1.4426950408889634
VMEM_LIMIT = 56 * 1024 * 1024

TOKEN_TILE = 512
FFN_TOKEN_TILE = 256
QUERY_TILE = 256
FFN_CHUNK = 256
FFN_LOOKAHEAD = 3
SAMPLE_PAGES_PER_STEP = 16


def _rms(x, g):
    return x * lax.rsqrt(jnp.mean(x * x, axis=-1, keepdims=True) + EPS) * g


def _softplus(z):
    return jnp.maximum(z, 0.0) + jnp.log(1.0 + jnp.exp2(jnp.abs(z) * -LOG2E))


def _gelu_tanh(x):
    return 0.5 * x * (1.0 + jnp.tanh(0.7978845608028654 * (x + 0.044715 * (x * x * x))))


def _const_spec(shape):
    nd = len(shape)
    return pl.BlockSpec(shape, lambda *_: (0,) * nd, pipeline_mode=pl.Buffered(1))


def _suffix_sum_matrix(n):
    return jnp.concatenate([_strict_lower(n), jnp.ones((n, n), BF16)], axis=1)


def _strict_lower(n):
    r = lax.broadcasted_iota(jnp.int32, (n, n), 0)
    c = lax.broadcasted_iota(jnp.int32, (n, n), 1)
    return (r > c).astype(BF16)


def _lower(n):
    r = lax.broadcasted_iota(jnp.int32, (n, n), 0)
    c = lax.broadcasted_iota(jnp.int32, (n, n), 1)
    return (r >= c).astype(BF16)


def _prompt_in_kernel(x_ref, g_ref, win_ref, poolw_ref, pscale_ref, sgw_ref, sgb_ref, wbr_ref,
                      q_ref, k_ref, v_ref, kb_ref, vb_ref, mab_ref, gc_ref, pool_ref,
                      ext_ref, *, tiles_per_seq):
    tm = x_ref.shape[0]
    i = pl.program_id(0)
    t0 = (i % tiles_per_seq) * tm

    @pl.when(i % tiles_per_seq == 0)
    def _():
        ext_ref[0:POOL_PREFIX, :] = jnp.zeros((POOL_PREFIX, ext_ref.shape[1]), F32)

    h = _rms(x_ref[...], g_ref[...]).astype(BF16)

    def proj(lo, hi):
        return jnp.dot(h, win_ref[:, lo:hi], preferred_element_type=F32)

    ext_ref[POOL_PREFIX:, :] = proj(0, 512)
    pos1 = (t0 + 1 + lax.broadcasted_iota(jnp.int32, (tm, LANES), 0)).astype(F32)
    ya = []
    for gi, w in enumerate(POOL_WINDOWS):
        sl = slice(gi * LANES, (gi + 1) * LANES)
        cur = ext_ref[POOL_PREFIX:POOL_PREFIX + tm, sl]
        s = cur
        for j in range(1, w):
            s = s + ext_ref[POOL_PREFIX - j:POOL_PREFIX - j + tm, sl]
        d = (s / jnp.minimum(pos1, float(w)) - cur).astype(BF16)
        ya.append(jnp.dot(d, poolw_ref[gi], preferred_element_type=F32) * pscale_ref[:, sl])
    ya = jnp.concatenate(ya, axis=1).astype(BF16)
    last = ext_ref[tm:tm + POOL_PREFIX, :]
    pool_ref[0] = last
    ext_ref[0:POOL_PREFIX, :] = last

    u = proj(512, 1024)
    sv = proj(1024, 1536).astype(BF16)
    r = lax.broadcasted_iota(jnp.int32, (CHUNK, CHUNK), 0)
    c = lax.broadcasted_iota(jnp.int32, (CHUNK, CHUNK), 1)
    cols = []
    for gi in range(4):
        sl = slice(gi * LANES, (gi + 1) * LANES)
        wg = jnp.where(r >= c, sgw_ref[gi], 0.0).astype(BF16)
        bcol = sgb_ref[:, gi:gi + 1]
        rows = []
        for ci in range(tm // CHUNK):
            rs = slice(ci * CHUNK, (ci + 1) * CHUNK)
            mixed = jnp.dot(wg, sv[rs, sl], preferred_element_type=F32) + bcol
            rows.append(u[rs, sl] * mixed)
        cols.append(jnp.concatenate(rows, axis=0))
    yb = jnp.concatenate(cols, axis=1).astype(BF16)

    q_ref[...] = (proj(1536, 2048) * SB_SCALE).astype(BF16)
    k = proj(2048, 2560)
    k_ref[0] = k.T
    kb_ref[...] = k.astype(BF16)
    v = proj(2560, 3072)
    v_ref[0] = v.T
    vb_ref[...] = v.astype(BF16)

    ga = jax.nn.sigmoid(proj(3072, 4096))
    gb = jax.nn.sigmoid(proj(4096, 5120))
    gc_ref[...] = jax.nn.sigmoid(proj(5120, 6144)).astype(BF16)
    mab_ref[...] = (ga * jnp.dot(ya, wbr_ref[0], preferred_element_type=F32)
                    + gb * jnp.dot(yb, wbr_ref[1], preferred_element_type=F32))


def _prompt_in(x2, g, win, poolw, pscale, sgw, sgb_t, wbr_ab, *, batch, seq):
    n, d = x2.shape
    tm = TOKEN_TILE
    tps = seq // tm
    row = lambda w: pl.BlockSpec((tm, w), lambda i: (i, 0))
    out_shape = (
        jax.ShapeDtypeStruct((n, SB_WIDTH), BF16),
        jax.ShapeDtypeStruct((batch, SB_WIDTH, seq), F32),
        jax.ShapeDtypeStruct((batch, SB_WIDTH, seq), F32),
        jax.ShapeDtypeStruct((n, SB_WIDTH), BF16),
        jax.ShapeDtypeStruct((n, SB_WIDTH), BF16),
        jax.ShapeDtypeStruct((n, d), F32),
        jax.ShapeDtypeStruct((n, d), BF16),
        jax.ShapeDtypeStruct((batch, POOL_PREFIX, 512), F32),
    )
    col = pl.BlockSpec((1, SB_WIDTH, tm), lambda i: (i // tps, 0, i % tps))
    out_specs = (row(SB_WIDTH), col, col, row(SB_WIDTH), row(SB_WIDTH),
                 row(d), row(d),
                 pl.BlockSpec((1, POOL_PREFIX, 512), lambda i: (i // tps, 0, 0)))
    return pl.pallas_call(
        functools.partial(_prompt_in_kernel, tiles_per_seq=tps),
        grid=(n // tm,),
        in_specs=[row(d), _const_spec(g.shape), _const_spec(win.shape), _const_spec(poolw.shape),
                  _const_spec(pscale.shape), _const_spec(sgw.shape), _const_spec(sgb_t.shape),
                  _const_spec(wbr_ab.shape)],
        out_specs=out_specs,
        out_shape=out_shape,
        scratch_shapes=[pltpu.VMEM((tm + POOL_PREFIX, 512), F32)],
        compiler_params=pltpu.CompilerParams(dimension_semantics=("arbitrary",),
                                             vmem_limit_bytes=VMEM_LIMIT),
        name="prompt_in",
    )(x2, g, win, poolw, pscale, sgw, sgb_t, wbr_ab)


def _prompt_attn_kernel(bias_ref, q_ref, kb_ref, vb_ref, tri_ref, mab_ref, gc_ref, x_ref,
                        wbrc_ref, wout_ref, g_ref, o_ref,
                        qm_ref, z_ref, spb_ref, ab_ref, acc_ref, carry_ref):
    tq = q_ref.shape[1]
    tk = tri_ref.shape[0]
    q0 = pl.program_id(1) * tq

    lane = lax.broadcasted_iota(jnp.int32, (tq, LANES), 1)
    low = lane < SB_HEAD_DIM
    for p in range(SB_HEADS // 2):
        qt = q_ref[0, :, p * LANES:(p + 1) * LANES]
        qm_ref[2 * p] = jnp.where(low, qt, 0.0).astype(BF16)
        qm_ref[2 * p + 1] = jnp.where(low, 0.0, qt).astype(BF16)
    acc_ref[...] = jnp.zeros(acc_ref.shape, F32)
    carry_ref[...] = jnp.zeros(carry_ref.shape, F32)

    def tile(ref, k0, h):
        return ref[0, pl.ds(k0, tk), (h // 2) * LANES:(h // 2 + 1) * LANES]

    def logits(k0, slot, h, mask):
        z = lax.dot_general(qm_ref[h], tile(kb_ref, k0, h), (((1,), (1,)), ((), ())),
                            preferred_element_type=F32) + bias_ref[h]
        sp = _softplus(z)
        if mask is not None:
            sp = jnp.where(mask, sp, 0.0)
            z = jnp.where(mask, z, -jnp.inf)
        z_ref[slot, h] = z
        spb_ref[slot, h] = sp.astype(BF16)

    def suffix(slot, h):
        return jnp.dot(spb_ref[slot, h], tri_ref[...], preferred_element_type=F32)

    def weights(slot, h, c):
        cr = carry_ref[h]
        w = z_ref[slot, h] - c - jnp.concatenate([cr] * (tk // LANES), axis=1)
        ab_ref[h] = jnp.exp(w.astype(BF16))
        carry_ref[h] = cr + jnp.broadcast_to(c[:, 0:1], cr.shape)

    def accumulate(k0, h):
        acc_ref[h] += jnp.dot(ab_ref[h], tile(vb_ref, k0, h), preferred_element_type=F32)

    def stage(apply_block, logits_block):
        c_prev = None
        for h in range(SB_HEADS + 1):
            if apply_block is not None:
                c = suffix(apply_block[1], h) if h < SB_HEADS else None
                if h >= 1:
                    weights(apply_block[1], h - 1, c_prev)
                    accumulate(apply_block[0], h - 1)
                c_prev = c
            if logits_block is not None and h < SB_HEADS:
                logits(logits_block[0], logits_block[1], h, None)

    def k_start(j):
        return pl.multiple_of(q0 - j * tk, tk)

    assert tq == tk
    n_apply = q0 // tk
    causal = (lax.broadcasted_iota(jnp.int32, (tq, tk), 1)
              < lax.broadcasted_iota(jnp.int32, (tq, tk), 0))
    for h in range(SB_HEADS):
        logits(k_start(0), 0, h, causal)

    def body(i, _):
        j = 2 * i
        stage((k_start(j), 0), (k_start(j + 1), 1))
        stage((k_start(j + 1), 1), (k_start(j + 2), 0))
        return 0

    lax.fori_loop(0, n_apply // 2, body, 0)

    @pl.when(n_apply % 2 == 1)
    def _():
        stage((k_start(n_apply - 1), 0), (k_start(n_apply), 1))
        stage((k_start(n_apply), 1), None)

    @pl.when(n_apply % 2 == 0)
    def _():
        stage((k_start(n_apply), 0), None)

    tiles = []
    for p in range(SB_HEADS // 2):
        tiles.append(jnp.where(low, acc_ref[2 * p], acc_ref[2 * p + 1]))
    yc = jnp.concatenate(tiles, axis=1).astype(BF16)
    m = mab_ref[0] + gc_ref[0].astype(F32) * jnp.dot(yc, wbrc_ref[...], preferred_element_type=F32)
    mixed = jnp.dot(m.astype(BF16), wout_ref[...], preferred_element_type=F32)
    o_ref[0] = x_ref[0] + _rms(mixed, g_ref[...])


def _prompt_attn(bias, q, kb, vb, tri, mab, gc, x, wbr_c, wout, g_post):
    b, s, d = x.shape
    tq = QUERY_TILE
    tk = tri.shape[0]
    row = lambda w: pl.BlockSpec((1, tq, w), lambda bi, qi: (bi, qi, 0))
    seq_spec = pl.BlockSpec((1, s, SB_WIDTH), lambda bi, qi: (bi, 0, 0), pipeline_mode=pl.Buffered(1))
    return pl.pallas_call(
        _prompt_attn_kernel,
        grid=(b, s // tq),
        in_specs=[pl.BlockSpec(memory_space=pltpu.SMEM),
                  row(SB_WIDTH), seq_spec, seq_spec, _const_spec(tri.shape),
                  row(d), row(d), row(d),
                  _const_spec(wbr_c.shape), _const_spec(wout.shape), _const_spec(g_post.shape)],
        out_specs=row(d),
        out_shape=jax.ShapeDtypeStruct((b, s, d), F32),
        scratch_shapes=[pltpu.VMEM((SB_HEADS, tq, LANES), BF16),
                        pltpu.VMEM((2, SB_HEADS, tq, tk), F32),
                        pltpu.VMEM((2, SB_HEADS, tq, tk), BF16),
                        pltpu.VMEM((SB_HEADS, tq, tk), BF16),
                        pltpu.VMEM((SB_HEADS, tq, LANES), F32),
                        pltpu.VMEM((SB_HEADS, tq, LANES), F32)],
        compiler_params=pltpu.CompilerParams(dimension_semantics=("arbitrary", "arbitrary"),
                                             vmem_limit_bytes=VMEM_LIMIT),
        name="prompt_attn",
    )(bias, q, kb, vb, tri, mab, gc, x, wbr_c, wout, g_post)


def _ffn_chunks(ffn_dim):
    assert ffn_dim % LANES == 0
    edges = list(range(0, ffn_dim, FFN_CHUNK)) + [ffn_dim]
    return list(zip(edges[:-1], edges[1:]))


def _prompt_ffn_kernel(x_ref, gpre_ref, wup_ref, cw_ref, cb_ref, wdown_ref, gpost_ref,
                       o_ref, conv_ref, ext_ref, *, tiles_per_seq):
    tm = x_ref.shape[0]
    ffn = wdown_ref.shape[0]
    i = pl.program_id(0)

    @pl.when(i % tiles_per_seq == 0)
    def _():
        ext_ref[0:CONV_PREFIX, :] = jnp.zeros((CONV_PREFIX, ext_ref.shape[1]), F32)

    x = x_ref[...]
    h = _rms(x, gpre_ref[...]).astype(BF16)

    def up(lo, hi):
        ext_ref[CONV_PREFIX:, lo:hi] = jnp.dot(h, wup_ref[:, lo:hi], preferred_element_type=F32)

    def conv(lo, hi):
        c = cb_ref[:, lo:hi]
        for j in range(CONV_WIDTH):
            s = CONV_PREFIX - (CONV_WIDTH - 1) + j
            c = c + ext_ref[s:s + tm, lo:hi] * cw_ref[j:j + 1, lo:hi]
        return c

    chunks = _ffn_chunks(ffn)
    f = jnp.zeros((tm, o_ref.shape[1]), F32)
    pending = None
    for i in range(len(chunks) + FFN_LOOKAHEAD + 1):
        if i < len(chunks):
            lo, hi = chunks[i]
            up(lo, hi)
            up(ffn + lo, ffn + hi)
        if pending is not None:
            act, lo, hi = pending
            f = f + jnp.dot(act, wdown_ref[lo:hi, :], preferred_element_type=F32)
            pending = None
        if FFN_LOOKAHEAD <= i < len(chunks) + FFN_LOOKAHEAD:
            lo, hi = chunks[i - FFN_LOOKAHEAD]
            act = _gelu_tanh(conv(lo, hi)) * conv(ffn + lo, ffn + hi)
            pending = (act.astype(BF16), lo, hi)
    o_ref[...] = x + _rms(f, gpost_ref[...])

    last = ext_ref[tm:tm + CONV_PREFIX, :]
    conv_ref[0] = last
    ext_ref[0:CONV_PREFIX, :] = last


def _prompt_ffn(x2, gpre, wup, cw, cb, wdown, gpost, *, batch, seq):
    n, d = x2.shape
    tm = FFN_TOKEN_TILE
    tps = seq // tm
    f2 = wup.shape[1]
    return pl.pallas_call(
        functools.partial(_prompt_ffn_kernel, tiles_per_seq=tps),
        grid=(n // tm,),
        in_specs=[pl.BlockSpec((tm, d), lambda i: (i, 0)),
                  _const_spec(gpre.shape), _const_spec(wup.shape), _const_spec(cw.shape),
                  _const_spec(cb.shape), _const_spec(wdown.shape), _const_spec(gpost.shape)],
        out_specs=(pl.BlockSpec((tm, d), lambda i: (i, 0)),
                   pl.BlockSpec((1, CONV_PREFIX, f2), lambda i: (i // tps, 0, 0))),
        out_shape=(jax.ShapeDtypeStruct((n, d), F32),
                   jax.ShapeDtypeStruct((batch, CONV_PREFIX, f2), F32)),
        scratch_shapes=[pltpu.VMEM((tm + CONV_PREFIX, f2), F32)],
        compiler_params=pltpu.CompilerParams(dimension_semantics=("arbitrary",),
                                             vmem_limit_bytes=VMEM_LIMIT),
        name="prompt_ffn",
    )(x2, gpre, wup, cw, cb, wdown, gpost)


def _sample_in_kernel(x_ref, g_ref, win_ref, pre_ref, poolw_ref, pscale_ref, sgw0_ref, sgb0_ref,
                      wbr_ref, a_ref, sgv_ref, q_ref, k_ref, v_ref, mab_ref, gc_ref, *, pos0):
    h = _rms(x_ref[...], g_ref[...]).astype(BF16)

    def proj(lo, hi):
        return jnp.dot(h, win_ref[:, lo:hi], preferred_element_type=F32)

    a = proj(0, 512)
    a_ref[...] = a
    ya = []
    for gi, w in enumerate(POOL_WINDOWS):
        sl = slice(gi * LANES, (gi + 1) * LANES)
        s = a[:, sl]
        for j in range(1, w):
            s = s + pre_ref[POOL_BUF - j, :, sl]
        d = (s / float(min(pos0 + 1, w)) - a[:, sl]).astype(BF16)
        ya.append(jnp.dot(d, poolw_ref[gi], preferred_element_type=F32) * pscale_ref[:, sl])
    ya = jnp.concatenate(ya, axis=1).astype(BF16)

    u = proj(512, 1024)
    sv = proj(1024, 1536)
    sgv_ref[...] = sv
    yb = (u * (sgw0_ref[...] * sv + sgb0_ref[...])).astype(BF16)

    q_ref[...] = proj(1536, 2048) * SB_SCALE
    k_ref[...] = proj(2048, 2560)
    v_ref[...] = proj(2560, 3072)
    ga = jax.nn.sigmoid(proj(3072, 4096))
    gb = jax.nn.sigmoid(proj(4096, 5120))
    gc_ref[...] = jax.nn.sigmoid(proj(5120, 6144))
    mab_ref[...] = (ga * jnp.dot(ya, wbr_ref[0], preferred_element_type=F32)
                    + gb * jnp.dot(yb, wbr_ref[1], preferred_element_type=F32))


def _sample_in(x, g, win, pre, poolw, pscale, sgw0, sgb0, wbr_ab, *, pos0):
    n, d = x.shape
    sd = lambda w: jax.ShapeDtypeStruct((n, w), F32)
    return pl.pallas_call(
        functools.partial(_sample_in_kernel, pos0=pos0),
        out_shape=(sd(512), sd(512), sd(SB_WIDTH), sd(SB_WIDTH), sd(SB_WIDTH), sd(d), sd(d)),
        compiler_params=pltpu.CompilerParams(vmem_limit_bytes=VMEM_LIMIT),
        name="sample_in",
    )(x, g, win, pre, poolw, pscale, sgw0, sgb0, wbr_ab)


def _sample_attn_kernel(pt_ref, q_ref, bias_ref, t2_ref, *refs, pages_per_step):
    del pt_ref
    k_refs = refs[:pages_per_step]
    v_refs = refs[pages_per_step:2 * pages_per_step]
    o_ref, carry_ref = refs[2 * pages_per_step:]

    @pl.when(pl.program_id(1) == 0)
    def _():
        o_ref[...] = jnp.zeros(o_ref.shape, F32)
        carry_ref[...] = jnp.zeros(carry_ref.shape, F32)

    q8 = q_ref[0].astype(BF16)
    row_k = lax.broadcasted_iota(jnp.int32, (SB_HEADS, KEY_BLOCK), 0)
    col_h = lax.broadcasted_iota(jnp.int32, (SB_HEAD_DIM, SB_HEADS), 1)

    logits = []
    for i in range(pages_per_step):
        z = jnp.zeros((SB_HEADS, KEY_BLOCK), F32)
        for h in range(SB_HEADS):
            res = jnp.dot(q8, k_refs[i][h].astype(BF16), preferred_element_type=F32)
            z = jnp.where(row_k == h, res, z)
        logits.append(z + bias_ref[...])
    softplus = [_softplus(z) for z in logits]
    sums = [jnp.dot(sp.astype(BF16), t2_ref[...], preferred_element_type=F32) for sp in softplus]
    carry = carry_ref[...]
    weights = []
    for z, sp, r in zip(logits, softplus, sums):
        weights.append(jnp.exp(z - sp - r[:, :KEY_BLOCK] - carry).astype(BF16))
        carry = carry + r[:, KEY_BLOCK:]
    carry_ref[...] = carry
    out = o_ref[0]
    for i in range(0, pages_per_step, 2):
        w2 = jnp.concatenate([weights[i], weights[i + 1]], axis=1)
        for h in range(SB_HEADS):
            v2 = jnp.concatenate([v_refs[i][h], v_refs[i + 1][h]], axis=1).astype(BF16)
            res = lax.dot_general(v2, w2, (((1,), (1,)), ((), ())), preferred_element_type=F32)
            out = out + jnp.where(col_h == h, res, 0.0)
    o_ref[0] = out


def _keys_minor(cache):
    return jnp.transpose(cache, (0, 1, 3, 4, 2))


def _sample_attn(page_table, q, bias_col, t2, cache_kt, cache_vt, *, layer):
    n, n_pages = page_table.shape
    assert cache_kt.shape[2:] == (SB_HEADS, SB_HEAD_DIM, KEY_BLOCK)
    pps = SAMPLE_PAGES_PER_STEP
    assert n_pages % pps == 0 and pps % 2 == 0

    def page_spec(i):
        return pl.BlockSpec((None, None, SB_HEADS, SB_HEAD_DIM, KEY_BLOCK),
                            lambda b, s, pt: (layer, pt[b, n_pages - 1 - (s * pps + i)], 0, 0, 0))

    kv_specs = [page_spec(i) for i in range(pps)]
    head_spec = pl.BlockSpec((1, SB_HEADS, SB_HEAD_DIM), lambda b, s, pt: (b, 0, 0))
    out_spec = pl.BlockSpec((1, SB_HEAD_DIM, SB_HEADS), lambda b, s, pt: (b, 0, 0))
    grid_spec = pltpu.PrefetchScalarGridSpec(
        num_scalar_prefetch=1,
        grid=(n, n_pages // pps),
        in_specs=[head_spec,
                  pl.BlockSpec(bias_col.shape, lambda b, s, pt: (0, 0)),
                  pl.BlockSpec(t2.shape, lambda b, s, pt: (0, 0))] + kv_specs + kv_specs,
        out_specs=out_spec,
        scratch_shapes=[pltpu.VMEM((SB_HEADS, KEY_BLOCK), F32)],
    )
    out = pl.pallas_call(
        functools.partial(_sample_attn_kernel, pages_per_step=pps),
        grid_spec=grid_spec,
        out_shape=jax.ShapeDtypeStruct((n, SB_HEAD_DIM, SB_HEADS), F32),
        compiler_params=pltpu.CompilerParams(dimension_semantics=("arbitrary", "arbitrary"),
                                             vmem_limit_bytes=VMEM_LIMIT),
        name="sample_attn",
    )(page_table, q.reshape(n, SB_HEADS, SB_HEAD_DIM), bias_col, t2,
      *([cache_kt] * pps), *([cache_vt] * pps))
    return jnp.transpose(out, (0, 2, 1)).reshape(n, SB_WIDTH)


def _sample_out_kernel(yc_ref, mab_ref, gc_ref, x_ref, wbrc_ref, wout_ref, gpost_ref, gpre_ref,
                       wup_ref, cw_ref, cb_ref, pre_ref, wdown_ref, gfpost_ref, o_ref, up_ref):
    ffn = wdown_ref.shape[0]
    m = mab_ref[...] + gc_ref[...] * jnp.dot(yc_ref[...].astype(BF16), wbrc_ref[...],
                                             preferred_element_type=F32)
    mixed = jnp.dot(m.astype(BF16), wout_ref[...], preferred_element_type=F32)
    x = x_ref[...] + _rms(mixed, gpost_ref[...])
    h = _rms(x, gpre_ref[...]).astype(BF16)

    def conv(lo, hi):
        up = jnp.dot(h, wup_ref[:, lo:hi], preferred_element_type=F32)
        up_ref[:, lo:hi] = up
        c = cb_ref[:, lo:hi] + up * cw_ref[CONV_WIDTH - 1:CONV_WIDTH, lo:hi]
        for j in range(CONV_WIDTH - 1):
            c = c + pre_ref[j, :, lo:hi] * cw_ref[j:j + 1, lo:hi]
        return c

    f = jnp.zeros(x.shape, F32)
    for lo, hi in _ffn_chunks(ffn):
        act = _gelu_tanh(conv(lo, hi)) * conv(ffn + lo, ffn + hi)
        f = f + jnp.dot(act.astype(BF16), wdown_ref[lo:hi, :], preferred_element_type=F32)
    o_ref[...] = x + _rms(f, gfpost_ref[...])


def _sample_out(yc, mab, gc, x, wbr_c, wout, gpost, gpre, wup, cw, cb, pre, wdown, gfpost):
    n, d = x.shape
    return pl.pallas_call(
        _sample_out_kernel,
        out_shape=(jax.ShapeDtypeStruct((n, d), F32),
                   jax.ShapeDtypeStruct((n, wup.shape[1]), F32)),
        compiler_params=pltpu.CompilerParams(vmem_limit_bytes=VMEM_LIMIT),
        name="sample_out",
    )(yc, mab, gc, x, wbr_c, wout, gpost, gpre, wup, cw, cb, pre, wdown, gfpost)


def kernel(x_prompt, x_sample, state_pool, cache_k, cache_v, page_table, state_conv,
           norm_mix_pre, norm_mix_post, norm_ffn_pre, norm_ffn_post, w_in, pool_w,
           pool_scale, sg_w, sg_b, sb_bias, w_branch, w_out, w_up, conv_w, conv_b, w_down):
    depth = w_in.shape[0]
    batch, seq, d = x_prompt.shape
    dec = x_sample.shape[0]
    page = cache_k.shape[2]
    past_len = page_table.shape[1] * page
    assert x_sample.shape[1] == 1 and past_len % CHUNK == 0 and page == KEY_BLOCK
    assert seq % TOKEN_TILE == 0 and seq % FFN_TOKEN_TILE == 0 and seq % QUERY_TILE == 0

    tri = _lower(QUERY_TILE)
    t2 = _suffix_sum_matrix(KEY_BLOCK)
    cache_kt = _keys_minor(cache_k)
    cache_vt = _keys_minor(cache_v)
    pool_pre = jnp.transpose(state_pool, (0, 2, 1, 3))
    conv_pre = jnp.transpose(state_conv, (0, 2, 1, 3))

    xp = x_prompt.reshape(batch * seq, d)
    xs = x_sample.reshape(dec, d)
    outs = [[] for _ in range(9)]
    for l in range(depth):
        row = lambda w: w[l][None, :]
        win = w_in[l].astype(BF16)
        poolw = pool_w[l].astype(BF16)
        wbr = w_branch[l].astype(BF16)
        wout = w_out[l].astype(BF16)
        wup = w_up[l].astype(BF16)
        wdown = w_down[l].astype(BF16)
        pscale = row(pool_scale)
        cb = row(conv_b)
        g_pre, g_post = row(norm_mix_pre), row(norm_mix_post)
        gf_pre, gf_post = row(norm_ffn_pre), row(norm_ffn_post)

        q, k, v, kb, vb, mab, gc, pool = _prompt_in(
            xp, g_pre, win, poolw, pscale, sg_w[l], sg_b[l].T, wbr[:2], batch=batch, seq=seq)
        sh = lambda t: t.reshape(batch, seq, t.shape[-1])
        x1 = _prompt_attn(sb_bias[l], sh(q), sh(kb), sh(vb), tri, sh(mab), sh(gc), sh(xp),
                          wbr[2], wout, g_post)
        xp, conv = _prompt_ffn(x1.reshape(batch * seq, d), gf_pre, wup, conv_w[l], cb, wdown,
                               gf_post, batch=batch, seq=seq)

        sgw0 = jnp.repeat(sg_w[l][:, 0, 0], LANES)[None, :]
        sgb0 = jnp.repeat(sg_b[l][:, 0], LANES)[None, :]
        a_s, sgv, q_s, k_s, v_s, mab_s, gc_s = _sample_in(
            xs, g_pre, win, pool_pre[l], poolw, pscale, sgw0, sgb0, wbr[:2], pos0=past_len)
        yc_s = _sample_attn(page_table, q_s, sb_bias[l][:, None], t2, cache_kt, cache_vt, layer=l)
        xs, up_s = _sample_out(yc_s, mab_s, gc_s, xs, wbr[2], wout, g_post,
                               gf_pre, wup, conv_w[l], cb, conv_pre[l], wdown, gf_post)

        outs[0].append(pool[:, 1:])
        outs[1].append(jnp.concatenate([state_pool[l][:, 1:], a_s[:, None]], axis=1))
        heads_last = lambda t: jnp.transpose(t.reshape(batch, SB_HEADS, SB_HEAD_DIM, seq), (0, 3, 1, 2))
        outs[2].append(heads_last(k))
        outs[3].append(heads_last(v))
        outs[4].append(k_s.reshape(dec, 1, SB_HEADS, SB_HEAD_DIM))
        outs[5].append(v_s.reshape(dec, 1, SB_HEADS, SB_HEAD_DIM))
        outs[6].append(sgv[:, None])
        outs[7].append(conv[:, CONV_PREFIX - (CONV_WIDTH - 1):])
        outs[8].append(jnp.concatenate([state_conv[l][:, 1:], up_s[:, None]], axis=1))

    return (xp.reshape(batch, seq, d), xs.reshape(dec, 1, d)) + tuple(jnp.stack(o) for o in outs)
```

```python
import functools

import jax
import jax.numpy as jnp
from jax import lax
from jax.experimental import pallas as pl
from jax.experimental.pallas import tpu as pltpu

F32 = jnp.float32
BF16 = jnp.bfloat16

LANES = 128
POOL_WINDOWS = (2, 4, 8, 16)
POOL_PREFIX = max(POOL_WINDOWS)
POOL_BUF = POOL_PREFIX - 1
CHUNK = 128
SB_HEADS = 8
SB_HEAD_DIM = 64
SB_WIDTH = SB_HEADS * SB_HEAD_DIM
SB_SCALE = SB_HEAD_DIM ** -0.5
KEY_BLOCK = 128
CONV_WIDTH = 3
CONV_PREFIX = 8
EPS = 1e-6
LOG2E = 1.4426950408889634
VMEM_LIMIT = 56 * 1024 * 1024

TOKEN_TILE = 512
FFN_TOKEN_TILE = 256
QUERY_TILE = 256
FFN_CHUNK = 256
FFN_LOOKAHEAD = 3
SAMPLE_PAGES_PER_STEP = 16


def _rms(x, g):
    return x * lax.rsqrt(jnp.mean(x * x, axis=-1, keepdims=True) + EPS) * g


def _softplus(z):
    return jnp.maximum(z, 0.0) + jnp.log(1.0 + jnp.exp2(jnp.abs(z) * -LOG2E))


def _gelu_tanh(x):
    return 0.5 * x * (1.0 + jnp.tanh(0.7978845608028654 * (x + 0.044715 * (x * x * x))))


def _const_spec(shape):
    nd = len(shape)
    return pl.BlockSpec(shape, lambda *_: (0,) * nd, pipeline_mode=pl.Buffered(1))


def _layer_spec(stacked, layer):
    nd = stacked.ndim
    return pl.BlockSpec((None,) + stacked.shape[1:], lambda *_: (layer,) + (0,) * (nd - 1),
                        pipeline_mode=pl.Buffered(1))


def _suffix_sum_matrix(n):
    return jnp.concatenate([_strict_lower(n), jnp.ones((n, n), BF16)], axis=1)


def _strict_lower(n):
    r = lax.broadcasted_iota(jnp.int32, (n, n), 0)
    c = lax.broadcasted_iota(jnp.int32, (n, n), 1)
    return (r > c).astype(BF16)


def _lower(n):
    r = lax.broadcasted_iota(jnp.int32, (n, n), 0)
    c = lax.broadcasted_iota(jnp.int32, (n, n), 1)
    return (r >= c).astype(BF16)


def _prompt_in_kernel(x_ref, g_ref, win_ref, poolw_ref, pscale_ref, sgw_ref, sgb_ref, wbr_ref,
                      q_ref, k_ref, v_ref, kb_ref, vb_ref, mab_ref, gc_ref, pool_ref,
                      ext_ref, *, tiles_per_seq):
    tm = x_ref.shape[0]
    i = pl.program_id(0)
    t0 = (i % tiles_per_seq) * tm

    @pl.when(i % tiles_per_seq == 0)
    def _():
        ext_ref[0:POOL_PREFIX, :] = jnp.zeros((POOL_PREFIX, ext_ref.shape[1]), F32)

    h = _rms(x_ref[...], g_ref[...]).astype(BF16)

    def proj(lo, hi):
        return jnp.dot(h, win_ref[:, lo:hi], preferred_element_type=F32)

    ext_ref[POOL_PREFIX:, :] = proj(0, 512)
    pos1 = (t0 + 1 + lax.broadcasted_iota(jnp.int32, (tm, LANES), 0)).astype(F32)
    ya = []
    for gi, w in enumerate(POOL_WINDOWS):
        sl = slice(gi * LANES, (gi + 1) * LANES)
        cur = ext_ref[POOL_PREFIX:POOL_PREFIX + tm, sl]
        s = cur
        for j in range(1, w):
            s = s + ext_ref[POOL_PREFIX - j:POOL_PREFIX - j + tm, sl]
        d = (s / jnp.minimum(pos1, float(w)) - cur).astype(BF16)
        ya.append(jnp.dot(d, poolw_ref[gi], preferred_element_type=F32) * pscale_ref[:, sl])
    ya = jnp.concatenate(ya, axis=1).astype(BF16)
    last = ext_ref[tm:tm + POOL_PREFIX, :]
    pool_ref[0] = last
    ext_ref[0:POOL_PREFIX, :] = last

    u = proj(512, 1024)
    sv = proj(1024, 1536).astype(BF16)
    r = lax.broadcasted_iota(jnp.int32, (CHUNK, CHUNK), 0)
    c = lax.broadcasted_iota(jnp.int32, (CHUNK, CHUNK), 1)
    cols = []
    for gi in range(4):
        sl = slice(gi * LANES, (gi + 1) * LANES)
        wg = jnp.where(r >= c, sgw_ref[gi], 0.0).astype(BF16)
        bcol = sgb_ref[:, gi:gi + 1]
        rows = []
        for ci in range(tm // CHUNK):
            rs = slice(ci * CHUNK, (ci + 1) * CHUNK)
            mixed = jnp.dot(wg, sv[rs, sl], preferred_element_type=F32) + bcol
            rows.append(u[rs, sl] * mixed)
        cols.append(jnp.concatenate(rows, axis=0))
    yb = jnp.concatenate(cols, axis=1).astype(BF16)

    q_ref[...] = (proj(1536, 2048) * SB_SCALE).astype(BF16)
    k = proj(2048, 2560)
    k_ref[0] = k.T
    kb_ref[...] = k.astype(BF16)
    v = proj(2560, 3072)
    v_ref[0] = v.T
    vb_ref[...] = v.astype(BF16)

    ga = jax.nn.sigmoid(proj(3072, 4096))
    gb = jax.nn.sigmoid(proj(4096, 5120))
    gc_ref[...] = jax.nn.sigmoid(proj(5120, 6144)).astype(BF16)
    mab_ref[...] = (ga * jnp.dot(ya, wbr_ref[0], preferred_element_type=F32)
                    + gb * jnp.dot(yb, wbr_ref[1], preferred_element_type=F32))


def _prompt_in(x2, g, win, poolw, pscale, sgw, sgb_t, wbr, *, layer, batch, seq):
    n, d = x2.shape
    tm = TOKEN_TILE
    tps = seq // tm
    row = lambda w: pl.BlockSpec((tm, w), lambda i: (i, 0))
    out_shape = (
        jax.ShapeDtypeStruct((n, SB_WIDTH), BF16),
        jax.ShapeDtypeStruct((batch, SB_WIDTH, seq), F32),
        jax.ShapeDtypeStruct((batch, SB_WIDTH, seq), F32),
        jax.ShapeDtypeStruct((n, SB_WIDTH), BF16),
        jax.ShapeDtypeStruct((n, SB_WIDTH), BF16),
        jax.ShapeDtypeStruct((n, d), F32),
        jax.ShapeDtypeStruct((n, d), BF16),
        jax.ShapeDtypeStruct((batch, POOL_PREFIX, 512), F32),
    )
    col = pl.BlockSpec((1, SB_WIDTH, tm), lambda i: (i // tps, 0, i % tps))
    out_specs = (row(SB_WIDTH), col, col, row(SB_WIDTH), row(SB_WIDTH),
                 row(d), row(d),
                 pl.BlockSpec((1, POOL_PREFIX, 512), lambda i: (i // tps, 0, 0)))
    return pl.pallas_call(
        functools.partial(_prompt_in_kernel, tiles_per_seq=tps),
        grid=(n // tm,),
        in_specs=[row(d), _const_spec(g.shape), _layer_spec(win, layer), _const_spec(poolw.shape),
                  _const_spec(pscale.shape), _const_spec(sgw.shape), _const_spec(sgb_t.shape),
                  _layer_spec(wbr, layer)],
        out_specs=out_specs,
        out_shape=out_shape,
        scratch_shapes=[pltpu.VMEM((tm + POOL_PREFIX, 512), F32)],
        compiler_params=pltpu.CompilerParams(dimension_semantics=("arbitrary",),
                                             vmem_limit_bytes=VMEM_LIMIT),
        name="prompt_in",
    )(x2, g, win, poolw, pscale, sgw, sgb_t, wbr)


def _prompt_attn_kernel(bias_ref, q_ref, kb_ref, vb_ref, tri_ref, mab_ref, gc_ref, x_ref,
                        wbrc_ref, wout_ref, g_ref, o_ref,
                        qm_ref, z_ref, spb_ref, ab_ref, acc_ref, carry_ref):
    tq = q_ref.shape[1]
    tk = tri_ref.shape[0]
    q0 = pl.program_id(1) * tq

    lane = lax.broadcasted_iota(jnp.int32, (tq, LANES), 1)
    low = lane < SB_HEAD_DIM
    for p in range(SB_HEADS // 2):
        qt = q_ref[0, :, p * LANES:(p + 1) * LANES]
        qm_ref[2 * p] = jnp.where(low, qt, 0.0).astype(BF16)
        qm_ref[2 * p + 1] = jnp.where(low, 0.0, qt).astype(BF16)
    acc_ref[...] = jnp.zeros(acc_ref.shape, F32)
    carry_ref[...] = jnp.zeros(carry_ref.shape, F32)

    def tile(ref, k0, h):
        return ref[0, pl.ds(k0, tk), (h // 2) * LANES:(h // 2 + 1) * LANES]

    def logits(k0, slot, h, mask):
        z = lax.dot_general(qm_ref[h], tile(kb_ref, k0, h), (((1,), (1,)), ((), ())),
                            preferred_element_type=F32) + bias_ref[h]
        sp = _softplus(z)
        if mask is not None:
            sp = jnp.where(mask, sp, 0.0)
            z = jnp.where(mask, z, -jnp.inf)
        z_ref[slot, h] = z
        spb_ref[slot, h] = sp.astype(BF16)

    def suffix(slot, h):
        return jnp.dot(spb_ref[slot, h], tri_ref[...], preferred_element_type=F32)

    def weights(slot, h, c):
        cr = carry_ref[h]
        w = z_ref[slot, h] - c - jnp.concatenate([cr] * (tk // LANES), axis=1)
        ab_ref[h] = jnp.exp(w.astype(BF16))
        carry_ref[h] = cr + jnp.broadcast_to(c[:, 0:1], cr.shape)

    def accumulate(k0, h):
        acc_ref[h] += jnp.dot(ab_ref[h], tile(vb_ref, k0, h), preferred_element_type=F32)

    def stage(apply_block, logits_block):
        c_prev = None
        for h in range(SB_HEADS + 1):
            if apply_block is not None:
                c = suffix(apply_block[1], h) if h < SB_HEADS else None
                if h >= 1:
                    weights(apply_block[1], h - 1, c_prev)
                    accumulate(apply_block[0], h - 1)
                c_prev = c
            if logits_block is not None and h < SB_HEADS:
                logits(logits_block[0], logits_block[1], h, None)

    def k_start(j):
        return pl.multiple_of(q0 - j * tk, tk)

    assert tq == tk
    n_apply = q0 // tk
    causal = (lax.broadcasted_iota(jnp.int32, (tq, tk), 1)
              < lax.broadcasted_iota(jnp.int32, (tq, tk), 0))
    for h in range(SB_HEADS):
        logits(k_start(0), 0, h, causal)

    def body(i, _):
        j = 2 * i
        stage((k_start(j), 0), (k_start(j + 1), 1))
        stage((k_start(j + 1), 1), (k_start(j + 2), 0))
        return 0

    lax.fori_loop(0, n_apply // 2, body, 0)

    @pl.when(n_apply % 2 == 1)
    def _():
        stage((k_start(n_apply - 1), 0), (k_start(n_apply), 1))
        stage((k_start(n_apply), 1), None)

    @pl.when(n_apply % 2 == 0)
    def _():
        stage((k_start(n_apply), 0), None)

    tiles = []
    for p in range(SB_HEADS // 2):
        tiles.append(jnp.where(low, acc_ref[2 * p], acc_ref[2 * p + 1]))
    yc = jnp.concatenate(tiles, axis=1).astype(BF16)
    m = mab_ref[0] + gc_ref[0].astype(F32) * jnp.dot(yc, wbrc_ref[2], preferred_element_type=F32)
    mixed = jnp.dot(m.astype(BF16), wout_ref[...], preferred_element_type=F32)
    o_ref[0] = x_ref[0] + _rms(mixed, g_ref[...])


def _prompt_attn(bias, q, kb, vb, tri, mab, gc, x, wbr, wout, g_post, *, layer):
    b, s, d = x.shape
    tq = QUERY_TILE
    tk = tri.shape[0]
    row = lambda w: pl.BlockSpec((1, tq, w), lambda bi, qi: (bi, qi, 0))
    seq_spec = pl.BlockSpec((1, s, SB_WIDTH), lambda bi, qi: (bi, 0, 0), pipeline_mode=pl.Buffered(1))
    return pl.pallas_call(
        _prompt_attn_kernel,
        grid=(b, s // tq),
        in_specs=[pl.BlockSpec(memory_space=pltpu.SMEM),
                  row(SB_WIDTH), seq_spec, seq_spec, _const_spec(tri.shape),
                  row(d), row(d), row(d),
                  _layer_spec(wbr, layer), _layer_spec(wout, layer), _const_spec(g_post.shape)],
        out_specs=row(d),
        out_shape=jax.ShapeDtypeStruct((b, s, d), F32),
        scratch_shapes=[pltpu.VMEM((SB_HEADS, tq, LANES), BF16),
                        pltpu.VMEM((2, SB_HEADS, tq, tk), F32),
                        pltpu.VMEM((2, SB_HEADS, tq, tk), BF16),
                        pltpu.VMEM((SB_HEADS, tq, tk), BF16),
                        pltpu.VMEM((SB_HEADS, tq, LANES), F32),
                        pltpu.VMEM((SB_HEADS, tq, LANES), F32)],
        compiler_params=pltpu.CompilerParams(dimension_semantics=("arbitrary", "arbitrary"),
                                             vmem_limit_bytes=VMEM_LIMIT),
        name="prompt_attn",
    )(bias, q, kb, vb, tri, mab, gc, x, wbr, wout, g_post)


def _ffn_chunks(ffn_dim):
    assert ffn_dim % LANES == 0
    edges = list(range(0, ffn_dim, FFN_CHUNK)) + [ffn_dim]
    return list(zip(edges[:-1], edges[1:]))


def _prompt_ffn_kernel(x_ref, gpre_ref, wup_ref, cw_ref, cb_ref, wdown_ref, gpost_ref,
                       o_ref, conv_ref, ext_ref, *, tiles_per_seq):
    tm = x_ref.shape[0]
    ffn = wdown_ref.shape[0]
    i = pl.program_id(0)

    @pl.when(i % tiles_per_seq == 0)
    def _():
        ext_ref[0:CONV_PREFIX, :] = jnp.zeros((CONV_PREFIX, ext_ref.shape[1]), F32)

    x = x_ref[...]
    h = _rms(x, gpre_ref[...]).astype(BF16)

    def up(lo, hi):
        ext_ref[CONV_PREFIX:, lo:hi] = jnp.dot(h, wup_ref[:, lo:hi], preferred_element_type=F32)

    def conv(lo, hi):
        c = cb_ref[:, lo:hi]
        for j in range(CONV_WIDTH):
            s = CONV_PREFIX - (CONV_WIDTH - 1) + j
            c = c + ext_ref[s:s + tm, lo:hi] * cw_ref[j:j + 1, lo:hi]
        return c

    chunks = _ffn_chunks(ffn)
    f = jnp.zeros((tm, o_ref.shape[1]), F32)
    pending = None
    for i in range(len(chunks) + FFN_LOOKAHEAD + 1):
        if i < len(chunks):
            lo, hi = chunks[i]
            up(lo, hi)
            up(ffn + lo, ffn + hi)
        if pending is not None:
            act, lo, hi = pending
            f = f + jnp.dot(act, wdown_ref[lo:hi, :], preferred_element_type=F32)
            pending = None
        if FFN_LOOKAHEAD <= i < len(chunks) + FFN_LOOKAHEAD:
            lo, hi = chunks[i - FFN_LOOKAHEAD]
            act = _gelu_tanh(conv(lo, hi)) * conv(ffn + lo, ffn + hi)
            pending = (act.astype(BF16), lo, hi)
    o_ref[...] = x + _rms(f, gpost_ref[...])

    last = ext_ref[tm:tm + CONV_PREFIX, :]
    conv_ref[0] = last
    ext_ref[0:CONV_PREFIX, :] = last


def _prompt_ffn(x2, gpre, wup, cw, cb, wdown, gpost, *, layer, batch, seq):
    n, d = x2.shape
    tm = FFN_TOKEN_TILE
    tps = seq // tm
    f2 = wup.shape[2]
    return pl.pallas_call(
        functools.partial(_prompt_ffn_kernel, tiles_per_seq=tps),
        grid=(n // tm,),
        in_specs=[pl.BlockSpec((tm, d), lambda i: (i, 0)),
                  _const_spec(gpre.shape), _layer_spec(wup, layer), _const_spec(cw.shape),
                  _const_spec(cb.shape), _layer_spec(wdown, layer), _const_spec(gpost.shape)],
        out_specs=(pl.BlockSpec((tm, d), lambda i: (i, 0)),
                   pl.BlockSpec((1, CONV_PREFIX, f2), lambda i: (i // tps, 0, 0))),
        out_shape=(jax.ShapeDtypeStruct((n, d), F32),
                   jax.ShapeDtypeStruct((batch, CONV_PREFIX, f2), F32)),
        scratch_shapes=[pltpu.VMEM((tm + CONV_PREFIX, f2), F32)],
        compiler_params=pltpu.CompilerParams(dimension_semantics=("arbitrary",),
                                             vmem_limit_bytes=VMEM_LIMIT),
        name="prompt_ffn",
    )(x2, gpre, wup, cw, cb, wdown, gpost)


def _sample_in_kernel(x_ref, g_ref, win_ref, pre_ref, poolw_ref, pscale_ref, sgw0_ref, sgb0_ref,
                      wbr_ref, a_ref, sgv_ref, q_ref, k_ref, v_ref, mab_ref, gc_ref, *, pos0):
    h = _rms(x_ref[...], g_ref[...]).astype(BF16)

    def proj(lo, hi):
        return jnp.dot(h, win_ref[:, lo:hi], preferred_element_type=F32)

    a = proj(0, 512)
    a_ref[...] = a
    ya = []
    for gi, w in enumerate(POOL_WINDOWS):
        sl = slice(gi * LANES, (gi + 1) * LANES)
        s = a[:, sl]
        for j in range(1, w):
            s = s + pre_ref[POOL_BUF - j, :, sl]
        d = (s / float(min(pos0 + 1, w)) - a[:, sl]).astype(BF16)
        ya.append(jnp.dot(d, poolw_ref[gi], preferred_element_type=F32) * pscale_ref[:, sl])
    ya = jnp.concatenate(ya, axis=1).astype(BF16)

    u = proj(512, 1024)
    sv = proj(1024, 1536)
    sgv_ref[...] = sv
    yb = (u * (sgw0_ref[...] * sv + sgb0_ref[...])).astype(BF16)

    q_ref[...] = proj(1536, 2048) * SB_SCALE
    k_ref[...] = proj(2048, 2560)
    v_ref[...] = proj(2560, 3072)
    ga = jax.nn.sigmoid(proj(3072, 4096))
    gb = jax.nn.sigmoid(proj(4096, 5120))
    gc_ref[...] = jax.nn.sigmoid(proj(5120, 6144))
    mab_ref[...] = (ga * jnp.dot(ya, wbr_ref[0], preferred_element_type=F32)
                    + gb * jnp.dot(yb, wbr_ref[1], preferred_element_type=F32))


def _whole_specs(args, layered, layer):
    return [_layer_spec(a, layer) if i in layered else _const_spec(a.shape) for i, a in enumerate(args)]


def _sample_in(x, g, win, pre, poolw, pscale, sgw0, sgb0, wbr, *, layer, pos0):
    n, d = x.shape
    sd = lambda w: jax.ShapeDtypeStruct((n, w), F32)
    args = (x, g, win, pre, poolw, pscale, sgw0, sgb0, wbr)
    out_shape = (sd(512), sd(512), sd(SB_WIDTH), sd(SB_WIDTH), sd(SB_WIDTH), sd(d), sd(d))
    return pl.pallas_call(
        functools.partial(_sample_in_kernel, pos0=pos0),
        grid=(1,),
        in_specs=_whole_specs(args, (2, 8), layer),
        out_specs=tuple(pl.BlockSpec(s.shape, lambda i: (0, 0)) for s in out_shape),
        out_shape=out_shape,
        compiler_params=pltpu.CompilerParams(dimension_semantics=("arbitrary",),
                                             vmem_limit_bytes=VMEM_LIMIT),
        name="sample_in",
    )(*args)


def _sample_attn_kernel(pt_ref, q_ref, bias_ref, t2_ref, *refs, pages_per_step):
    del pt_ref
    k_refs = refs[:pages_per_step]
    v_refs = refs[pages_per_step:2 * pages_per_step]
    o_ref, carry_ref = refs[2 * pages_per_step:]

    @pl.when(pl.program_id(1) == 0)
    def _():
        o_ref[...] = jnp.zeros(o_ref.shape, F32)
        carry_ref[...] = jnp.zeros(carry_ref.shape, F32)

    q8 = q_ref[0].astype(BF16)
    row_k = lax.broadcasted_iota(jnp.int32, (SB_HEADS, KEY_BLOCK), 0)
    col_h = lax.broadcasted_iota(jnp.int32, (SB_HEAD_DIM, SB_HEADS), 1)

    logits = []
    for i in range(pages_per_step):
        z = jnp.zeros((SB_HEADS, KEY_BLOCK), F32)
        for h in range(SB_HEADS):
            res = jnp.dot(q8, k_refs[i][h].astype(BF16), preferred_element_type=F32)
            z = jnp.where(row_k == h, res, z)
        logits.append(z + bias_ref[...])
    softplus = [_softplus(z) for z in logits]
    sums = [jnp.dot(sp.astype(BF16), t2_ref[...], preferred_element_type=F32) for sp in softplus]
    carry = carry_ref[...]
    weights = []
    for z, sp, r in zip(logits, softplus, sums):
        weights.append(jnp.exp(z - sp - r[:, :KEY_BLOCK] - carry).astype(BF16))
        carry = carry + r[:, KEY_BLOCK:]
    carry_ref[...] = carry
    out = o_ref[0]
    for i in range(0, pages_per_step, 2):
        w2 = jnp.concatenate([weights[i], weights[i + 1]], axis=1)
        for h in range(SB_HEADS):
            v2 = jnp.concatenate([v_refs[i][h], v_refs[i + 1][h]], axis=1).astype(BF16)
            res = lax.dot_general(v2, w2, (((1,), (1,)), ((), ())), preferred_element_type=F32)
            out = out + jnp.where(col_h == h, res, 0.0)
    o_ref[0] = out


def _keys_minor(cache):
    return jnp.transpose(cache, (0, 1, 3, 4, 2))


def _sample_attn(page_table, q, bias_col, t2, cache_kt, cache_vt, *, layer):
    n, n_pages = page_table.shape
    assert cache_kt.shape[2:] == (SB_HEADS, SB_HEAD_DIM, KEY_BLOCK)
    pps = SAMPLE_PAGES_PER_STEP
    assert n_pages % pps == 0 and pps % 2 == 0

    def page_spec(i):
        return pl.BlockSpec((None, None, SB_HEADS, SB_HEAD_DIM, KEY_BLOCK),
                            lambda b, s, pt: (layer, pt[b, n_pages - 1 - (s * pps + i)], 0, 0, 0))

    kv_specs = [page_spec(i) for i in range(pps)]
    head_spec = pl.BlockSpec((1, SB_HEADS, SB_HEAD_DIM), lambda b, s, pt: (b, 0, 0))
    out_spec = pl.BlockSpec((1, SB_HEAD_DIM, SB_HEADS), lambda b, s, pt: (b, 0, 0))
    grid_spec = pltpu.PrefetchScalarGridSpec(
        num_scalar_prefetch=1,
        grid=(n, n_pages // pps),
        in_specs=[head_spec,
                  pl.BlockSpec(bias_col.shape, lambda b, s, pt: (0, 0)),
                  pl.BlockSpec(t2.shape, lambda b, s, pt: (0, 0))] + kv_specs + kv_specs,
        out_specs=out_spec,
        scratch_shapes=[pltpu.VMEM((SB_HEADS, KEY_BLOCK), F32)],
    )
    out = pl.pallas_call(
        functools.partial(_sample_attn_kernel, pages_per_step=pps),
        grid_spec=grid_spec,
        out_shape=jax.ShapeDtypeStruct((n, SB_HEAD_DIM, SB_HEADS), F32),
        compiler_params=pltpu.CompilerParams(dimension_semantics=("arbitrary", "arbitrary"),
                                             vmem_limit_bytes=VMEM_LIMIT),
        name="sample_attn",
    )(page_table, q.reshape(n, SB_HEADS, SB_HEAD_DIM), bias_col, t2,
      *([cache_kt] * pps), *([cache_vt] * pps))
    return jnp.transpose(out, (0, 2, 1)).reshape(n, SB_WIDTH)


def _sample_out_kernel(yc_ref, mab_ref, gc_ref, x_ref, wbrc_ref, wout_ref, gpost_ref, gpre_ref,
                       wup_ref, cw_ref, cb_ref, pre_ref, wdown_ref, gfpost_ref, o_ref, up_ref):
    ffn = wdown_ref.shape[0]
    m = mab_ref[...] + gc_ref[...] * jnp.dot(yc_ref[...].astype(BF16), wbrc_ref[2],
                                             preferred_element_type=F32)
    mixed = jnp.dot(m.astype(BF16), wout_ref[...], preferred_element_type=F32)
    x = x_ref[...] + _rms(mixed, gpost_ref[...])
    h = _rms(x, gpre_ref[...]).astype(BF16)

    def conv(lo, hi):
        up = jnp.dot(h, wup_ref[:, lo:hi], preferred_element_type=F32)
        up_ref[:, lo:hi] = up
        c = cb_ref[:, lo:hi] + up * cw_ref[CONV_WIDTH - 1:CONV_WIDTH, lo:hi]
        for j in range(CONV_WIDTH - 1):
            c = c + pre_ref[j, :, lo:hi] * cw_ref[j:j + 1, lo:hi]
        return c

    f = jnp.zeros(x.shape, F32)
    for lo, hi in _ffn_chunks(ffn):
        act = _gelu_tanh(conv(lo, hi)) * conv(ffn + lo, ffn + hi)
        f = f + jnp.dot(act.astype(BF16), wdown_ref[lo:hi, :], preferred_element_type=F32)
    o_ref[...] = x + _rms(f, gfpost_ref[...])


def _sample_out(yc, mab, gc, x, wbr, wout, gpost, gpre, wup, cw, cb, pre, wdown, gfpost, *, layer):
    n, d = x.shape
    args = (yc, mab, gc, x, wbr, wout, gpost, gpre, wup, cw, cb, pre, wdown, gfpost)
    out_shape = (jax.ShapeDtypeStruct((n, d), F32),
                 jax.ShapeDtypeStruct((n, wup.shape[2]), F32))
    return pl.pallas_call(
        _sample_out_kernel,
        grid=(1,),
        in_specs=_whole_specs(args, (4, 5, 8, 12), layer),
        out_specs=tuple(pl.BlockSpec(s.shape, lambda i: (0, 0)) for s in out_shape),
        out_shape=out_shape,
        compiler_params=pltpu.CompilerParams(dimension_semantics=("arbitrary",),
                                             vmem_limit_bytes=VMEM_LIMIT),
        name="sample_out",
    )(*args)


def kernel(x_prompt, x_sample, state_pool, cache_k, cache_v, page_table, state_conv,
           norm_mix_pre, norm_mix_post, norm_ffn_pre, norm_ffn_post, w_in, pool_w,
           pool_scale, sg_w, sg_b, sb_bias, w_branch, w_out, w_up, conv_w, conv_b, w_down):
    depth = w_in.shape[0]
    batch, seq, d = x_prompt.shape
    dec = x_sample.shape[0]
    page = cache_k.shape[2]
    past_len = page_table.shape[1] * page
    assert x_sample.shape[1] == 1 and past_len % CHUNK == 0 and page == KEY_BLOCK
    assert seq % TOKEN_TILE == 0 and seq % FFN_TOKEN_TILE == 0 and seq % QUERY_TILE == 0

    tri = _lower(QUERY_TILE)
    t2 = _suffix_sum_matrix(KEY_BLOCK)
    cache_kt = _keys_minor(cache_k)
    cache_vt = _keys_minor(cache_v)
    pool_pre = jnp.transpose(state_pool, (0, 2, 1, 3))
    conv_pre = jnp.transpose(state_conv, (0, 2, 1, 3))

    xp = x_prompt.reshape(batch * seq, d)
    xs = x_sample.reshape(dec, d)
    outs = [[] for _ in range(9)]
    win, wbr, wout = w_in.astype(BF16), w_branch.astype(BF16), w_out.astype(BF16)
    wup, wdown = w_up.astype(BF16), w_down.astype(BF16)
    for l in range(depth):
        row = lambda w: w[l][None, :]
        poolw = pool_w[l].astype(BF16)
        pscale = row(pool_scale)
        cb = row(conv_b)
        g_pre, g_post = row(norm_mix_pre), row(norm_mix_post)
        gf_pre, gf_post = row(norm_ffn_pre), row(norm_ffn_post)

        q, k, v, kb, vb, mab, gc, pool = _prompt_in(
            xp, g_pre, win, poolw, pscale, sg_w[l], sg_b[l].T, wbr, layer=l, batch=batch, seq=seq)
        sh = lambda t: t.reshape(batch, seq, t.shape[-1])
        x1 = _prompt_attn(sb_bias[l], sh(q), sh(kb), sh(vb), tri, sh(mab), sh(gc), sh(xp),
                          wbr, wout, g_post, layer=l)
        xp, conv = _prompt_ffn(x1.reshape(batch * seq, d), gf_pre, wup, conv_w[l], cb, wdown,
                               gf_post, layer=l, batch=batch, seq=seq)

        sgw0 = jnp.repeat(sg_w[l][:, 0, 0], LANES)[None, :]
        sgb0 = jnp.repeat(sg_b[l][:, 0], LANES)[None, :]
        a_s, sgv, q_s, k_s, v_s, mab_s, gc_s = _sample_in(
            xs, g_pre, win, pool_pre[l], poolw, pscale, sgw0, sgb0, wbr, layer=l, pos0=past_len)
        yc_s = _sample_attn(page_table, q_s, sb_bias[l][:, None], t2, cache_kt, cache_vt, layer=l)
        xs, up_s = _sample_out(yc_s, mab_s, gc_s, xs, wbr, wout, g_post,
                               gf_pre, wup, conv_w[l], cb, conv_pre[l], wdown, gf_post, layer=l)

        outs[0].append(pool[:, 1:])
        outs[1].append(jnp.concatenate([state_pool[l][:, 1:], a_s[:, None]], axis=1))
        heads_last = lambda t: jnp.transpose(t.reshape(batch, SB_HEADS, SB_HEAD_DIM, seq), (0, 3, 1, 2))
        outs[2].append(heads_last(k))
        outs[3].append(heads_last(v))
        outs[4].append(k_s.reshape(dec, 1, SB_HEADS, SB_HEAD_DIM))
        outs[5].append(v_s.reshape(dec, 1, SB_HEADS, SB_HEAD_DIM))
        outs[6].append(sgv[:, None])
        outs[7].append(conv[:, CONV_PREFIX - (CONV_WIDTH - 1):])
        outs[8].append(jnp.concatenate([state_conv[l][:, 1:], up_s[:, None]], axis=1))

    return (xp.reshape(batch, seq, d), xs.reshape(dec, 1, d)) + tuple(jnp.stack(o) for o in outs)
```
